```python
import jax, jax.numpy as jnp
from jax import lax
import numpy as np

D_MODEL = 1024
BATCH = 8
SEQ = 2048
DEPTH = 1
DEC_BATCH = 128
DEC_SEQ = 4
PAST_LEN = 16384
PAGE_SIZE = 128

D_MIX = D_MODEL
D_CONV = D_MIX // 2
CONV_GROUPS = 8
CONV_W = 3
D_MLSTM = D_MIX - D_CONV
MLSTM_HEADS = 4
MLSTM_DH = D_MLSTM // MLSTM_HEADS
MLSTM_CHUNK = 64
D_IN = 3 * D_CONV + 4 * D_MLSTM + 2 * MLSTM_HEADS
PEER_HEADS = 8
PEER_NKEYS = 128
PEER_NEXP = PEER_NKEYS * PEER_NKEYS
PEER_DKEY = 256
PEER_DHALF = PEER_DKEY // 2
PEER_TOPK = 16
PEER_BLOCK = 256
EPS = 1e-6

kernel_name = "hybrid_conv_mlstm_peer_adaln_step"


def group_rmsnorm(x, n_groups, g):
    shp = x.shape
    xf = x.astype(jnp.float32).reshape(shp[:-1] + (n_groups, shp[-1] // n_groups))
    y = xf * lax.rsqrt(jnp.mean(xf * xf, axis=-1, keepdims=True) + EPS)
    return (y.reshape(shp) * g).astype(x.dtype)


def rmsnorm(x, g):
    return group_rmsnorm(x, 1, g)


def mlstm_chunk(carry, inp):
    C, n, m = carry
    q, k, v, li, lf = inp
    L = q.shape[2]
    b = jnp.cumsum(lf, axis=-1)
    a = b + m[..., None]
    causal = jnp.tril(jnp.ones((L, L), dtype=bool))
    dmat = jnp.where(causal, b[..., :, None] - b[..., None, :] + li[..., None, :], -jnp.inf)
    mt = jnp.maximum(a, jnp.max(dmat, axis=-1))
    w_inter = jnp.exp(a - mt)
    s = jnp.einsum('bhtd,bhsd->bhts', q, k) * jnp.exp(dmat - mt[..., None])
    num = w_inter[..., None] * jnp.einsum('bhtk,bhkv->bhtv', q, C) + jnp.einsum('bhts,bhsv->bhtv', s, v)
    den = w_inter * jnp.einsum('bhtk,bhk->bht', q, n) + jnp.sum(s, axis=-1)
    h = num / jnp.maximum(jnp.abs(den), jnp.exp(-mt))[..., None]
    bl = b[..., -1]
    gl = bl[..., None] - b + li
    m_new = jnp.maximum(bl + m, jnp.max(gl, axis=-1))
    w_old = jnp.exp(bl + m - m_new)
    w_s = jnp.exp(gl - m_new[..., None])
    C_new = w_old[..., None, None] * C + jnp.einsum('bhs,bhsk,bhsv->bhkv', w_s, k, v)
    n_new = w_old[..., None] * n + jnp.einsum('bhs,bhsk->bhk', w_s, k)
    return (C_new, n_new, m_new), h


def mlstm_sequence(q, k, v, li, lf, state):
    L = q.shape[2]
    chunk = MLSTM_CHUNK if (L % MLSTM_CHUNK == 0) else L
    if chunk == L:
        return mlstm_chunk(state, (q, k, v, li, lf))
    nc = L // chunk
    def to_chunks(t):
        t = t.reshape(t.shape[:2] + (nc, chunk) + t.shape[3:])
        return jnp.moveaxis(t, 2, 0)
    xs = (to_chunks(q), to_chunks(k), to_chunks(v), to_chunks(li), to_chunks(lf))
    new_state, hs = lax.scan(mlstm_chunk, state, xs)
    hs = jnp.moveaxis(hs, 0, 2)
    return new_state, hs.reshape(hs.shape[:2] + (L, hs.shape[-1]))


def token_mixer(h, conv_state, C0, n0, m0, w_in, w_conv, b_i, b_f, g_conv, g_mlstm, w_out):
    bsz, L, _ = h.shape
    z = h @ w_in
    sizes = [D_CONV] * 3 + [D_MLSTM] * 4 + [MLSTM_HEADS] * 2
    splits = [int(s) for s in np.cumsum(sizes)[:-1]]
    xc, bg, cg, q, k, v, o, ig, fg = jnp.split(z, splits, axis=-1)
    u = cg * xc
    full = jnp.concatenate([conv_state.astype(u.dtype), u], axis=1)
    conv = w_conv[0] * full[:, 0:L]
    for j in range(1, CONV_W):
        conv = conv + w_conv[j] * full[:, j:j + L]
    y_conv = bg * conv
    new_conv = full[:, L:]
    def heads(t):
        return t.reshape(bsz, L, MLSTM_HEADS, MLSTM_DH).transpose(0, 2, 1, 3).astype(jnp.float32)
    qh = heads(q)
    kh = heads(k) * (MLSTM_DH ** -0.5)
    vh = heads(v)
    li = (ig + b_i).astype(jnp.float32).transpose(0, 2, 1)
    lf = jax.nn.log_sigmoid((fg + b_f).astype(jnp.float32)).transpose(0, 2, 1)
    state = (C0.astype(jnp.float32), n0.astype(jnp.float32), m0.astype(jnp.float32))
    (Cn, nn_, mn), hm = mlstm_sequence(qh, kh, vh, li, lf, state)
    hm = hm.transpose(0, 2, 1, 3).reshape(bsz, L, D_MLSTM)
    hm = jax.nn.sigmoid(o.astype(jnp.float32)) * hm
    y_m = group_rmsnorm(hm, MLSTM_HEADS, g_mlstm).astype(h.dtype)
    y = jnp.concatenate([group_rmsnorm(y_conv, CONV_GROUPS, g_conv), y_m], axis=-1) @ w_out
    return y, new_conv, Cn, nn_, mn


def peer(h, w_pq, sub_keys, u_tab, v_tab):
    bsz, L, D = h.shape
    T = bsz * L
    nb = -(-T // PEER_BLOCK)
    xf = jnp.pad(h.reshape(T, D), ((0, nb * PEER_BLOCK - T), (0, 0))).reshape(nb, PEER_BLOCK, D)
    def block(xb):
        q = (xb @ w_pq).reshape(PEER_BLOCK, PEER_HEADS, 2, PEER_DHALF).astype(jnp.float32)
        s = jnp.einsum('thpd,phnd->thpn', q, sub_keys.astype(jnp.float32))
        s1, i1 = lax.top_k(s[:, :, 0], PEER_TOPK)
        s2, i2 = lax.top_k(s[:, :, 1], PEER_TOPK)
        cand = (s1[..., :, None] + s2[..., None, :]).reshape(PEER_BLOCK, PEER_HEADS, PEER_TOPK * PEER_TOPK)
        cidx = (i1[..., :, None] * PEER_NKEYS + i2[..., None, :]).reshape(PEER_BLOCK, PEER_HEADS, PEER_TOPK * PEER_TOPK)
        top, pos = lax.top_k(cand, PEER_TOPK)
        eidx = jnp.take_along_axis(cidx, pos, axis=-1)
        gate = jax.nn.softmax(top, axis=-1)
        a = jnp.einsum('thkd,td->thk', u_tab[eidx], xb).astype(jnp.float32)
        act = (jax.nn.gelu(a) * gate).astype(xb.dtype)
        return jnp.einsum('thk,thkd->td', act, v_tab[eidx])
    out = lax.map(block, xf).reshape(nb * PEER_BLOCK, D)[:T]
    return out.reshape(bsz, L, D)


def run_stack(x, c, st_conv, st_C, st_n, st_m, w_mod, b_mod, g_mix, w_in, w_conv, b_i, b_f,
              g_conv, g_mlstm, w_out, g_ffn, w_pq, sub_keys, u_tab, v_tab, g_final):
    convs, Cs, ns, ms = [], [], [], []
    for l in range(DEPTH):
        mod = jax.nn.silu(c) @ w_mod[l] + b_mod[l]
        sh1, sc1, gt1, sh2, sc2, gt2 = jnp.split(mod[:, None, :], 6, axis=-1)
        h = rmsnorm(x, g_mix[l]) * (1 + sc1) + sh1
        mix, cv, Cn, nn_, mn = token_mixer(h, st_conv[l], st_C[l], st_n[l], st_m[l], w_in[l], w_conv[l],
                                         b_i[l], b_f[l], g_conv[l], g_mlstm[l], w_out[l])
        x = x + gt1 * mix
        h = rmsnorm(x, g_ffn[l]) * (1 + sc2) + sh2
        x = x + gt2 * peer(h, w_pq[l], sub_keys[l], u_tab[l], v_tab[l])
        convs.append(cv); Cs.append(Cn); ns.append(nn_); ms.append(mn)
    return rmsnorm(x, g_final), jnp.stack(convs), jnp.stack(Cs), jnp.stack(ns), jnp.stack(ms)


def setup_inputs(seed: int = 0) -> dict:
    key = jax.random.key(seed)
    ks = jax.random.split(key, 32)
    nrm = jax.random.normal
    f32 = jnp.float32
    D = D_MODEL
    return {
        "x_prompt": nrm(ks[0], (BATCH, SEQ, D), f32),
        "x_sample": nrm(ks[1], (DEC_BATCH, DEC_SEQ, D), f32),
        "c_prompt": nrm(ks[2], (BATCH, D), f32),
        "c_sample": nrm(ks[3], (DEC_BATCH, D), f32),
        "state_conv": 0.5 * nrm(ks[4], (DEPTH, DEC_BATCH, CONV_W - 1, D_CONV), f32),
        "state_mlstm_C": 0.1 * nrm(ks[5], (DEPTH, DEC_BATCH, MLSTM_HEADS, MLSTM_DH, MLSTM_DH), f32),
        "state_mlstm_n": 0.1 * nrm(ks[6], (DEPTH, DEC_BATCH, MLSTM_HEADS, MLSTM_DH), f32),
        "state_mlstm_m": 0.5 * nrm(ks[7], (DEPTH, DEC_BATCH, MLSTM_HEADS), f32),
        "w_mod": 0.5 * D ** -0.5 * nrm(ks[8], (DEPTH, D, 6 * D), f32),
        "b_mod": 0.02 * nrm(ks[9], (DEPTH, 6 * D), f32),
        "g_mix": 1.0 + 0.02 * nrm(ks[10], (DEPTH, D), f32),
        "w_in": D ** -0.5 * nrm(ks[11], (DEPTH, D, D_IN), f32),
        "w_conv": CONV_W ** -0.5 * nrm(ks[12], (DEPTH, CONV_W, D_CONV), f32),
        "b_i": 0.1 * nrm(ks[13], (DEPTH, MLSTM_HEADS), f32),
        "b_f": jnp.linspace(3.0, 6.0, MLSTM_HEADS, dtype=f32)[None, :] + 0.1 * nrm(ks[14], (DEPTH, MLSTM_HEADS), f32),
        "g_conv": 1.0 + 0.02 * nrm(ks[15], (DEPTH, D_CONV), f32),
        "g_mlstm": 1.0 + 0.02 * nrm(ks[16], (DEPTH, D_MLSTM), f32),
        "w_out": D_MIX ** -0.5 * nrm(ks[17], (DEPTH, D_MIX, D), f32),
        "g_ffn": 1.0 + 0.02 * nrm(ks[18], (DEPTH, D), f32),
        "w_pq": D ** -0.5 * nrm(ks[19], (DEPTH, D, PEER_HEADS * PEER_DKEY), f32),
        "sub_keys": PEER_DHALF ** -0.5 * nrm(ks[20], (DEPTH, 2, PEER_HEADS, PEER_NKEYS, PEER_DHALF), f32),
        "u_tab": D ** -0.5 * nrm(ks[21], (DEPTH, PEER_NEXP, D), f32),
        "v_tab": nrm(ks[22], (DEPTH, PEER_NEXP, D), f32),
        "g_final": 1.0 + 0.02 * nrm(ks[23], (D,), f32),
    }


def reference(x_prompt, x_sample, c_prompt, c_sample, state_conv, state_mlstm_C, state_mlstm_n, state_mlstm_m,
              w_mod, b_mod, g_mix, w_in, w_conv, b_i, b_f, g_conv, g_mlstm, w_out, g_ffn, w_pq, sub_keys,
              u_tab, v_tab, g_final):
    weights = (w_mod, b_mod, g_mix, w_in, w_conv, b_i, b_f, g_conv, g_mlstm, w_out, g_ffn, w_pq, sub_keys,
               u_tab, v_tab, g_final)
    bp = x_prompt.shape[0]
    z_conv = jnp.zeros((DEPTH, bp, CONV_W - 1, D_CONV), x_prompt.dtype)
    z_C = jnp.zeros((DEPTH, bp, MLSTM_HEADS, MLSTM_DH, MLSTM_DH), jnp.float32)
    z_n = jnp.zeros((DEPTH, bp, MLSTM_HEADS, MLSTM_DH), jnp.float32)
    z_m = jnp.zeros((DEPTH, bp, MLSTM_HEADS), jnp.float32)
    y_prompt, conv_p, C_p, n_p, m_p = run_stack(x_prompt, c_prompt, z_conv, z_C, z_n, z_m, *weights)
    y_sample, conv_s, C_s, n_s, m_s = run_stack(x_sample, c_sample, state_conv, state_mlstm_C, state_mlstm_n,
                                                state_mlstm_m, *weights)
    return (y_prompt, y_sample, conv_p, C_p, n_p, m_p, conv_s, C_s, n_s, m_s)
```

```python
import functools

import jax
import jax.numpy as jnp
import numpy as np
from jax import lax
from jax.experimental import pallas as pl
from jax.experimental.pallas import tpu as pltpu

F32 = jnp.float32
BF16 = jnp.bfloat16
NEG_INF = float("-inf")
EPS = 1e-6

D_MODEL = 1024
D_CONV = 512
CONV_GROUPS = 8
D_MLSTM = 512
HEADS = 4
DH = 128
D_MAIN = 3 * D_CONV + 4 * D_MLSTM
PEER_HEADS = 8
NKEYS = 128
TOPK = 16
NEXP = NKEYS * NKEYS

VMEM_LIMIT = 56 * 1024 * 1024

PREMIX_ROWS = 256
CHUNK = 256
POST_ROWS = 256
ROUTER_COLS = 512
EXPERT_COLS = 512
EXPERT_ROWS = 1024
SAMPLE_GROUP = 32


def _dot(a, b):
    return jnp.dot(a, b, preferred_element_type=F32)


def _dot_nt(a, b):
    return lax.dot_general(a, b, (((1,), (1,)), ((), ())), preferred_element_type=F32)


def _dot_tn(a, b):
    return lax.dot_general(a, b, (((0,), (0,)), ((), ())), preferred_element_type=F32)


def _dot_f32(a, b):
    return jnp.dot(a, b, precision=lax.Precision.HIGHEST, preferred_element_type=F32)


def _params(*sem):
    return pltpu.CompilerParams(dimension_semantics=sem, vmem_limit_bytes=VMEM_LIMIT)


def _rms(x):
    return x * lax.rsqrt(jnp.mean(x * x, axis=-1, keepdims=True) + EPS)


def _mod_kernel(c_ref, w_ref, b_ref, o_ref):
    c = c_ref[...]
    a = (c * jax.nn.sigmoid(c)).astype(BF16)
    o_ref[...] = _dot(a, w_ref[...].astype(BF16)) + b_ref[...]


def _mod_rows(c, w_mod, b_mod):
    n = c.shape[0]
    tn = 1024
    return pl.pallas_call(
        _mod_kernel,
        grid=(w_mod.shape[1] // tn,),
        in_specs=[pl.BlockSpec((n, D_MODEL), lambda j: (0, 0)),
                  pl.BlockSpec((D_MODEL, tn), lambda j: (0, j)),
                  pl.BlockSpec((1, tn), lambda j: (0, j))],
        out_specs=pl.BlockSpec((n, tn), lambda j: (0, j)),
        out_shape=jax.ShapeDtypeStruct((n, w_mod.shape[1]), F32),
        compiler_params=_params("arbitrary"),
        name="mod_rows",
    )(c, w_mod, b_mod.reshape(1, -1))


def _log_sigmoid(x):
    return jnp.minimum(x, 0.0) - jnp.log1p(jnp.exp(-jnp.abs(x)))


def _premix_core(x, sh, sc, gmix, win_ref, wg_ref, bif):
    h = _rms(x) * gmix * (1.0 + sc) + sh
    z = _dot(h.astype(BF16), win_ref[...])
    pre = _dot_f32(h, wg_ref[...]) + bif
    lane = lax.broadcasted_iota(jnp.int32, pre.shape, 1)
    gates = jnp.where(lane < HEADS, pre, _log_sigmoid(pre))
    return z, gates


def _conv_tail(z, um1, um2, u, wconv, gconv, gmat_ref, ycn_ref, qkv_ref, og_ref):
    bg = z[:, D_CONV:2 * D_CONV]
    conv = wconv[0:1] * um2 + wconv[1:2] * um1 + wconv[2:3] * u
    yc = bg * conv
    ysq = yc * yc
    hi = ysq.astype(BF16)
    lo = (ysq - hi.astype(F32)).astype(BF16)
    gsum = _dot(hi, gmat_ref[...]) + _dot(lo, gmat_ref[...])
    ycn_ref[...] = (yc * lax.rsqrt(gsum * (CONV_GROUPS / D_CONV) + EPS) * gconv).astype(BF16)
    o0 = 3 * D_CONV
    qkv_ref[:, 0:D_MLSTM] = z[:, o0:o0 + D_MLSTM].astype(BF16)
    qkv_ref[:, D_MLSTM:2 * D_MLSTM] = (z[:, o0 + D_MLSTM:o0 + 2 * D_MLSTM] * (DH ** -0.5)).astype(BF16)
    qkv_ref[:, 2 * D_MLSTM:3 * D_MLSTM] = z[:, o0 + 2 * D_MLSTM:o0 + 3 * D_MLSTM].astype(BF16)
    og_ref[...] = jax.nn.sigmoid(z[:, o0 + 3 * D_MLSTM:o0 + 4 * D_MLSTM]).astype(BF16)


def _premix_prompt_kernel(x_ref, sh_ref, sc_ref, gmix_ref, win_ref, wg_ref, wconv_ref, bif_ref, gconv_ref,
                          gmat_ref, ycn_ref, qkv_ref, og_ref, gate_ref, tail_ref, carry_ref):
    rows = x_ref.shape[0]

    @pl.when(pl.program_id(1) == 0)
    def _():
        carry_ref[...] = jnp.zeros_like(carry_ref)

    z, gates = _premix_core(x_ref[...], sh_ref[0], sc_ref[0], gmix_ref[...], win_ref, wg_ref, bif_ref[...])
    gate_ref[...] = gates
    u = z[:, 2 * D_CONV:3 * D_CONV] * z[:, 0:D_CONV]
    prev = carry_ref[...]
    p0, p1 = prev[6:7], prev[7:8]
    ri = lax.broadcasted_iota(jnp.int32, (rows, 1), 0)
    um1 = jnp.where(ri == 0, p1, pltpu.roll(u, 1, 0))
    um2 = jnp.where(ri == 0, p0, jnp.where(ri == 1, p1, pltpu.roll(u, 2, 0)))
    carry_ref[...] = u[rows - 8:rows]
    tail_ref[0] = u[rows - 8:rows]
    _conv_tail(z, um1, um2, u, wconv_ref[...], gconv_ref[...], gmat_ref, ycn_ref, qkv_ref, og_ref)


def _premix_sample_kernel(x_ref, sh_ref, sc_ref, gmix_ref, win_ref, wg_ref, wconv_ref, bif_ref, gconv_ref,
                          gmat_ref, sm1_ref, sm2_ref, ycn_ref, qkv_ref, og_ref, gate_ref, u_ref, *, seq):
    rows = x_ref.shape[0]
    z, gates = _premix_core(x_ref[...], sh_ref[...], sc_ref[...], gmix_ref[...], win_ref, wg_ref, bif_ref[...])
    gate_ref[...] = gates
    u = z[:, 2 * D_CONV:3 * D_CONV] * z[:, 0:D_CONV]
    u_ref[...] = u
    tmod = lax.broadcasted_iota(jnp.int32, (rows, 1), 0) % seq
    um1 = jnp.where(tmod == 0, sm1_ref[...], pltpu.roll(u, 1, 0))
    um2 = jnp.where(tmod < 2, sm2_ref[...], pltpu.roll(u, 2, 0))
    _conv_tail(z, um1, um2, u, wconv_ref[...], gconv_ref[...], gmat_ref, ycn_ref, qkv_ref, og_ref)


def _const_spec(shape):
    nd = len(shape)
    return pl.BlockSpec(shape, lambda *_: (0,) * nd)


def _premix_weights(w):
    return [w["gmix"], w["win"], w["wg"], w["wconv"], w["bif"], w["gconv"], w["gmat"]]


def _premix_weight_specs():
    return [_const_spec((1, D_MODEL)), _const_spec((D_MODEL, D_MAIN)), _const_spec((D_MODEL, 8)),
            _const_spec((3, D_CONV)), _const_spec((1, 8)), _const_spec((1, D_CONV)),
            _const_spec((D_CONV, D_CONV))]


def _premix_prompt(x, sh, sc, w, nb, seq):
    t = x.shape[0]
    rows = PREMIX_ROWS
    nl = seq // rows
    tok = lambda b, l: (b * nl + l, 0)
    per_seq = pl.BlockSpec((1, 1, D_MODEL), lambda b, l: (b, 0, 0))
    return pl.pallas_call(
        _premix_prompt_kernel,
        grid=(nb, nl),
        in_specs=[pl.BlockSpec((rows, D_MODEL), tok), per_seq, per_seq] + _premix_weight_specs(),
        out_specs=[pl.BlockSpec((rows, D_CONV), tok), pl.BlockSpec((rows, 3 * D_MLSTM), tok),
                   pl.BlockSpec((rows, D_MLSTM), tok), pl.BlockSpec((rows, 8), tok),
                   pl.BlockSpec((1, 8, D_CONV), lambda b, l: (b, 0, 0))],
        out_shape=[jax.ShapeDtypeStruct((t, D_CONV), BF16), jax.ShapeDtypeStruct((t, 3 * D_MLSTM), BF16),
                   jax.ShapeDtypeStruct((t, D_MLSTM), BF16), jax.ShapeDtypeStruct((t, 8), F32),
                   jax.ShapeDtypeStruct((nb, 8, D_CONV), F32)],
        scratch_shapes=[pltpu.VMEM((8, D_CONV), F32)],
        compiler_params=_params("arbitrary", "arbitrary"),
        name="premix_prompt",
    )(x, sh, sc, *_premix_weights(w))


def _premix_sample(x, sh, sc, sm1, sm2, w, seq):
    t = x.shape[0]
    full = lambda n: _const_spec((t, n))
    return pl.pallas_call(
        functools.partial(_premix_sample_kernel, seq=seq),
        grid=(1,),
        in_specs=[full(D_MODEL), full(D_MODEL), full(D_MODEL)] + _premix_weight_specs()
        + [full(D_CONV), full(D_CONV)],
        out_specs=[full(D_CONV), full(3 * D_MLSTM), full(D_MLSTM), full(8), full(D_CONV)],
        out_shape=[jax.ShapeDtypeStruct((t, D_CONV), BF16), jax.ShapeDtypeStruct((t, 3 * D_MLSTM), BF16),
                   jax.ShapeDtypeStruct((t, D_MLSTM), BF16), jax.ShapeDtypeStruct((t, 8), F32),
                   jax.ShapeDtypeStruct((t, D_CONV), F32)],
        compiler_params=_params("arbitrary"),
        name="premix_sample",
    )(x, sh, sc, *_premix_weights(w), sm1, sm2)


def _mlstm_chunk_kernel(qkv_ref, og_ref, gate_ref, gml_ref, h_ref, c_ref, n_ref, m_ref, c_scr, n_scr, m_scr):
    lc = qkv_ref.shape[0]

    @pl.when(pl.program_id(1) == 0)
    def _():
        c_scr[...] = jnp.zeros_like(c_scr)
        n_scr[...] = jnp.zeros_like(n_scr)
        m_scr[...] = jnp.zeros_like(m_scr)

    gates = gate_ref[...]
    eye = (lax.broadcasted_iota(jnp.int32, (8, 8), 0) == lax.broadcasted_iota(jnp.int32, (8, 8), 1)).astype(F32)
    grow = lax.dot_general(eye, gates, (((1,), (1,)), ((), ())), precision=lax.Precision.HIGHEST,
                           preferred_element_type=F32)
    r = lax.broadcasted_iota(jnp.int32, (lc, lc), 0)
    c = lax.broadcasted_iota(jnp.int32, (lc, lc), 1)
    causal = c <= r
    tril = causal.astype(F32)
    bcol = _dot_f32(tril, gates)
    brow = lax.dot_general(grow, tril, (((1,), (1,)), ((), ())), precision=lax.Precision.HIGHEST,
                           preferred_element_type=F32)

    for h in range(HEADS):
        q = qkv_ref[:, h * DH:(h + 1) * DH]
        k = qkv_ref[:, D_MLSTM + h * DH:D_MLSTM + (h + 1) * DH]
        v = qkv_ref[:, 2 * D_MLSTM + h * DH:2 * D_MLSTM + (h + 1) * DH]
        li_c = gates[:, h:h + 1]
        li_r = grow[h:h + 1, :]
        b_c = bcol[:, HEADS + h:HEADS + h + 1]
        b_r = brow[HEADS + h:HEADS + h + 1, :]
        m_old = m_scr[h][0:1, 0:1]
        c_old = c_scr[h]
        n_old = n_scr[h][0:1, :]

        a = b_c + m_old
        dmat = jnp.where(causal, b_c - b_r + li_r, NEG_INF)
        mt = jnp.maximum(a, jnp.max(dmat, axis=1, keepdims=True))
        s = _dot_nt(q, k) * jnp.exp(dmat - mt)
        w_inter = jnp.exp(a - mt)
        num = w_inter * _dot(q, c_old.astype(BF16)) + _dot(s.astype(BF16), v)
        den = w_inter * jnp.sum(q.astype(F32) * n_old, axis=1, keepdims=True) + jnp.sum(s, axis=1, keepdims=True)
        hh = num / jnp.maximum(jnp.abs(den), jnp.exp(-mt))
        hh = og_ref[:, h * DH:(h + 1) * DH].astype(F32) * hh
        h_ref[:, h * DH:(h + 1) * DH] = (_rms(hh) * gml_ref[:, h * DH:(h + 1) * DH]).astype(BF16)

        bl = b_c[lc - 1:lc, :]
        gl = bl - b_c + li_c
        m_new = jnp.maximum(bl + m_old, jnp.max(gl, axis=0, keepdims=True))
        w_old = jnp.exp(bl + m_old - m_new)
        kw = k.astype(F32) * jnp.exp(gl - m_new)
        c_new = w_old * c_old + _dot_tn(kw.astype(BF16), v)
        n_new = w_old * n_old + jnp.sum(kw, axis=0, keepdims=True)
        c_scr[h] = c_new
        n_scr[h] = jnp.broadcast_to(n_new, (8, DH))
        m_scr[h] = jnp.broadcast_to(m_new, (8, DH))
        c_ref[0, h] = c_new
        n_ref[0, h] = jnp.broadcast_to(n_new, (8, DH))
        m_ref[0, h] = jnp.broadcast_to(m_new, (8, DH))


def _mlstm_prompt(qkv, og, gates, gml, nb, seq):
    t = qkv.shape[0]
    nc = seq // CHUNK
    tok = lambda b, c: (b * nc + c, 0)
    state = lambda b, c: (b, 0, 0, 0)
    return pl.pallas_call(
        _mlstm_chunk_kernel,
        grid=(nb, nc),
        in_specs=[pl.BlockSpec((CHUNK, 3 * D_MLSTM), tok), pl.BlockSpec((CHUNK, D_MLSTM), tok),
                  pl.BlockSpec((CHUNK, 8), tok), _const_spec((1, D_MLSTM))],
        out_specs=[pl.BlockSpec((CHUNK, D_MLSTM), tok), pl.BlockSpec((1, HEADS, DH, DH), state),
                   pl.BlockSpec((1, HEADS, 8, DH), state), pl.BlockSpec((1, HEADS, 8, DH), state)],
        out_shape=[jax.ShapeDtypeStruct((t, D_MLSTM), BF16), jax.ShapeDtypeStruct((nb, HEADS, DH, DH), F32),
                   jax.ShapeDtypeStruct((nb, HEADS, 8, DH), F32), jax.ShapeDtypeStruct((nb, HEADS, 8, DH), F32)],
        scratch_shapes=[pltpu.VMEM((HEADS, DH, DH), F32), pltpu.VMEM((HEADS, 8, DH), F32),
                        pltpu.VMEM((HEADS, 8, DH), F32)],
        compiler_params=_params("arbitrary", "arbitrary"),
        name="mlstm_prompt",
    )(qkv, og, gates, gml)


def _mlstm_step_kernel(q_ref, qt_ref, k_ref, kt_ref, v_ref, og_ref, li_ref, lf_ref, gml_ref, c0_ref, n0_ref,
                       m0_ref, h_ref, c_ref, n_ref, m_ref, *, seq):
    c = c0_ref[...]
    n = n0_ref[...]
    m = m0_ref[...]
    for t in range(seq):
        lit = li_ref[:, :, t:t + 1]
        lft = lf_ref[:, :, t:t + 1]
        m_new = jnp.maximum(lft + m, lit)
        fw = jnp.exp(lft + m - m_new)
        iw = jnp.exp(lit - m_new)
        c = fw * c + (iw * kt_ref[:, :, t:t + 1]) * v_ref[:, t:t + 1, :]
        n = fw * n + iw * k_ref[:, t:t + 1, :]
        num = jnp.sum(qt_ref[:, :, t:t + 1] * c, axis=1, keepdims=True)
        den = jnp.sum(q_ref[:, t:t + 1, :] * n, axis=2, keepdims=True)
        hh = og_ref[:, t:t + 1, :] * (num / jnp.maximum(jnp.abs(den), jnp.exp(-m_new)))
        h_ref[:, t:t + 1, :] = _rms(hh) * gml_ref[...]
        m = m_new
    c_ref[...] = c
    n_ref[...] = n
    m_ref[...] = m


def _mlstm_sample(q, qt, k, kt, v, og, li, lf, gml, c0, n0, m0, seq):
    ng = q.shape[0]
    g = SAMPLE_GROUP
    blk = lambda *shape: pl.BlockSpec((g,) + shape, lambda i: (i,) + (0,) * len(shape))
    return pl.pallas_call(
        functools.partial(_mlstm_step_kernel, seq=seq),
        grid=(ng // g,),
        in_specs=[blk(seq, DH), blk(DH, seq), blk(seq, DH), blk(DH, seq), blk(seq, DH), blk(seq, DH),
                  blk(1, seq), blk(1, seq), blk(1, DH), blk(DH, DH), blk(1, DH), blk(1, 1)],
        out_specs=[blk(seq, DH), blk(DH, DH), blk(1, DH), blk(1, 1)],
        out_shape=[jax.ShapeDtypeStruct((ng, seq, DH), F32), jax.ShapeDtypeStruct((ng, DH, DH), F32),
                   jax.ShapeDtypeStruct((ng, 1, DH), F32), jax.ShapeDtypeStruct((ng, 1, 1), F32)],
        compiler_params=_params("arbitrary"),
        name="mlstm_sample",
    )(q, qt, k, kt, v, og, li, lf, gml, c0, n0, m0)


def _postmix_kernel(x_ref, ycn_ref, hmn_ref, gt_ref, sc_ref, sh_ref, wo_ref, gffn_ref, x1_ref, h2t_ref):
    y = _dot(ycn_ref[...], wo_ref[0:D_CONV, :]) + _dot(hmn_ref[...], wo_ref[D_CONV:D_CONV + D_MLSTM, :])
    x1 = x_ref[...] + gt_ref[0] * y
    x1_ref[...] = x1
    h2 = _rms(x1) * gffn_ref[...] * (1.0 + sc_ref[0]) + sh_ref[0]
    h2t_ref[...] = h2.T.astype(BF16)


def _postmix(x, ycn, hmn, gt, sc, sh, wo, gffn, rows_per_mod):
    t = x.shape[0]
    rows = POST_ROWS
    r = gt.shape[1]
    per = rows_per_mod // rows
    tok = lambda i: (i, 0)
    mod = pl.BlockSpec((1, r, D_MODEL), lambda i: (i // per, 0, 0))
    return pl.pallas_call(
        _postmix_kernel,
        grid=(t // rows,),
        in_specs=[pl.BlockSpec((rows, D_MODEL), tok), pl.BlockSpec((rows, D_CONV), tok),
                  pl.BlockSpec((rows, D_MLSTM), tok), mod, mod, mod,
                  _const_spec((D_MODEL, D_MODEL)), _const_spec((1, D_MODEL))],
        out_specs=[pl.BlockSpec((rows, D_MODEL), tok), pl.BlockSpec((D_MODEL, rows), lambda i: (0, i))],
        out_shape=[jax.ShapeDtypeStruct((t, D_MODEL), F32), jax.ShapeDtypeStruct((D_MODEL, t), BF16)],
        compiler_params=_params("arbitrary"),
        name="postmix",
    )(x, ycn, hmn, gt, sc, sh, wo, gffn)


def _top_values(s, count):
    cols = s.shape[1]
    rid = lax.broadcasted_iota(jnp.int32, (count, cols), 0)
    out = jnp.zeros((count, cols), F32)
    cur = s
    for r in range(count):
        mx = jnp.max(cur, axis=0, keepdims=True)
        out = jnp.where(rid == r, mx, out)
        cur = jnp.where(cur == mx, NEG_INF, cur)
    return out


def _router_kernel(h2t_ref, wpq_ref, keys_ref, s1_ref, s2_ref, c1_ref, e2_ref, tau_ref):
    ht = h2t_ref[...]
    cols = ht.shape[1]
    rid8 = lax.broadcasted_iota(jnp.int32, (8, cols), 0)

    def head(h, carry):
        row0 = pl.multiple_of(h * 2 * NKEYS, 2 * NKEYS)
        qt = _dot(wpq_ref[pl.ds(row0, 2 * NKEYS), :], ht).astype(BF16)
        s1 = _dot(keys_ref[h], qt[0:NKEYS])
        s2 = _dot(keys_ref[PEER_HEADS + h], qt[NKEYS:2 * NKEYS])
        t1 = _top_values(s1, TOPK)
        t2 = _top_values(s2, TOPK)
        parts = [t1[0:1] + t2]
        for a in range(1, 8):
            parts.append(jnp.where(rid8 < TOPK // (a + 1), t1[a:a + 1] + t2[0:8], NEG_INF))
        parts.append(t1[8:16] + t2[0:1])
        cands = jnp.concatenate(parts, axis=0)
        tau = _top_values(cands, TOPK)[TOPK - 1:TOPK]
        top = t1[0:1] + t2[0:1]
        z = jnp.sum(jnp.where(cands >= tau, jnp.exp(cands - top), 0.0), axis=0, keepdims=True)
        s1_ref[h] = s1
        s2_ref[h] = s2
        c1_ref[h] = jnp.exp(s1 - t1[0:1]) / z
        e2_ref[h] = jnp.exp(s2 - t2[0:1])
        tau_ref[h] = jnp.broadcast_to(tau, (8, cols))
        return carry

    lax.fori_loop(0, PEER_HEADS, head, 0)


def _router(h2t, wpq, keys):
    t = h2t.shape[1]
    cols = ROUTER_COLS
    big = pl.BlockSpec((PEER_HEADS, NKEYS, cols), lambda i: (0, 0, i))
    big_shape = jax.ShapeDtypeStruct((PEER_HEADS, NKEYS, t), F32)
    return pl.pallas_call(
        _router_kernel,
        grid=(t // cols,),
        in_specs=[pl.BlockSpec((D_MODEL, cols), lambda i: (0, i)),
                  _const_spec((PEER_HEADS * 2 * NKEYS, D_MODEL)),
                  _const_spec((2 * PEER_HEADS, NKEYS, NKEYS))],
        out_specs=[big, big, big, big, pl.BlockSpec((PEER_HEADS, 8, cols), lambda i: (0, 0, i))],
        out_shape=[big_shape, big_shape, big_shape, big_shape,
                   jax.ShapeDtypeStruct((PEER_HEADS, 8, t), F32)],
        compiler_params=_params("arbitrary"),
        name="peer_router",
    )(h2t, wpq, keys)


def _gelu_tanh(a):
    return 0.5 * a * (1.0 + jnp.tanh(0.7978845608028654 * (a + 0.044715 * (a * a * a))))


def _experts_kernel(h2t_ref, u_ref, vt_ref, s1_ref, c1_ref, s2_ref, e2_ref, tau_ref, o_ref, acc_ref, g_ref):
    e = pl.program_id(1)

    @pl.when(e == 0)
    def _():
        acc_ref[...] = jnp.zeros_like(acc_ref)

    ht = h2t_ref[...]
    for j in range(EXPERT_ROWS // NKEYS):
        a = _dot(u_ref[j * NKEYS:(j + 1) * NKEYS, :], ht)
        w = jnp.zeros_like(a)
        for h in range(PEER_HEADS):
            sel = (s1_ref[h, j:j + 1, :] + s2_ref[h]) >= tau_ref[h, 0:1, :]
            w = w + jnp.where(sel, e2_ref[h], 0.0) * c1_ref[h, j:j + 1, :]
        g_ref[j * NKEYS:(j + 1) * NKEYS, :] = (w * _gelu_tanh(a)).astype(BF16)
    acc_ref[...] += _dot(vt_ref[...], g_ref[...])

    @pl.when(e == pl.num_programs(1) - 1)
    def _():
        o_ref[...] = acc_ref[...].T


def _experts(h2t, u, vt, s1, c1, s2, e2, tau):
    t = h2t.shape[1]
    cols = EXPERT_COLS
    per_i1 = pl.BlockSpec((PEER_HEADS, EXPERT_ROWS // NKEYS, cols), lambda i, e: (0, e, i))
    per_i2 = pl.BlockSpec((PEER_HEADS, NKEYS, cols), lambda i, e: (0, 0, i))
    return pl.pallas_call(
        _experts_kernel,
        grid=(t // cols, NEXP // EXPERT_ROWS),
        in_specs=[pl.BlockSpec((D_MODEL, cols), lambda i, e: (0, i)),
                  pl.BlockSpec((EXPERT_ROWS, D_MODEL), lambda i, e: (e, 0)),
                  pl.BlockSpec((D_MODEL, EXPERT_ROWS), lambda i, e: (0, e)),
                  per_i1, per_i1, per_i2, per_i2,
                  pl.BlockSpec((PEER_HEADS, 8, cols), lambda i, e: (0, 0, i))],
        out_specs=pl.BlockSpec((cols, D_MODEL), lambda i, e: (i, 0)),
        out_shape=jax.ShapeDtypeStruct((t, D_MODEL), F32),
        scratch_shapes=[pltpu.VMEM((D_MODEL, cols), F32), pltpu.VMEM((EXPERT_ROWS, cols), BF16)],
        compiler_params=_params("arbitrary", "arbitrary"),
        name="peer_experts",
    )(h2t, u, vt, s1, c1, s2, e2, tau)


def _final_kernel(x1_ref, p_ref, gt_ref, gfin_ref, y_ref):
    y_ref[...] = _rms(x1_ref[...] + gt_ref[0] * p_ref[...]) * gfin_ref[...]


def _final(x1, p, gt, gfin, rows_per_mod):
    t = x1.shape[0]
    rows = POST_ROWS
    r = gt.shape[1]
    per = rows_per_mod // rows
    tok = pl.BlockSpec((rows, D_MODEL), lambda i: (i, 0))
    return pl.pallas_call(
        _final_kernel,
        grid=(t // rows,),
        in_specs=[tok, tok, pl.BlockSpec((1, r, D_MODEL), lambda i: (i // per, 0, 0)), _const_spec((1, D_MODEL))],
        out_specs=tok,
        out_shape=jax.ShapeDtypeStruct((t, D_MODEL), F32),
        compiler_params=_params("arbitrary"),
        name="final_norm",
    )(x1, p, gt, gfin)


def _group_matrix():
    g = np.arange(D_CONV) // (D_CONV // CONV_GROUPS)
    return jnp.asarray(g[:, None] == g[None, :], dtype=BF16)


def _peer_and_final(x1, h2t, gt2, w, rows_per_mod):
    s1, s2, c1, e2, tau = _router(h2t, w["wpq"], w["keys"])
    p = _experts(h2t, w["u"], w["vt"], s1, c1, s2, e2, tau)
    return _final(x1, p, gt2, w["gfin"], rows_per_mod)


def kernel(x_prompt, x_sample, c_prompt, c_sample, state_conv, state_mlstm_C, state_mlstm_n, state_mlstm_m,
           w_mod, b_mod, g_mix, w_in, w_conv, b_i, b_f, g_conv, g_mlstm, w_out, g_ffn, w_pq, sub_keys,
           u_tab, v_tab, g_final):
    depth = w_mod.shape[0]
    assert depth == 1
    nb, seq, _ = x_prompt.shape
    ns, sseq, _ = x_sample.shape
    ts = ns * sseq

    w = {
        "gmix": g_mix[0].reshape(1, -1),
        "win": w_in[0][:, :D_MAIN].astype(BF16),
        "wg": w_in[0][:, D_MAIN:],
        "wconv": w_conv[0],
        "bif": jnp.concatenate([b_i[0], b_f[0]]).reshape(1, 8),
        "gconv": g_conv[0].reshape(1, -1),
        "gmat": _group_matrix(),
        "gml": g_mlstm[0].reshape(1, -1),
        "wo": w_out[0].astype(BF16),
        "gffn": g_ffn[0].reshape(1, -1),
        "wpq": w_pq[0].T.astype(BF16),
        "keys": sub_keys[0].reshape(2 * PEER_HEADS, NKEYS, -1).astype(BF16),
        "u": u_tab[0].astype(BF16),
        "vt": v_tab[0].T.astype(BF16),
        "gfin": g_final.reshape(1, -1),
    }

    mod = _mod_rows(jnp.concatenate([c_prompt, c_sample], axis=0), w_mod[0], b_mod[0])
    mod_p = mod[:nb].reshape(nb, 1, 6, D_MODEL)
    sh1p, sc1p, gt1p, sh2p, sc2p, gt2p = (mod_p[:, :, i] for i in range(6))
    mod_s = jnp.broadcast_to(mod[nb:].reshape(ns, 1, 6, D_MODEL), (ns, sseq, 6, D_MODEL)).reshape(ts, 6, D_MODEL)
    sh1s, sc1s, gt1s, sh2s, sc2s, gt2s = (mod_s[:, i] for i in range(6))

    xp = x_prompt.reshape(nb * seq, D_MODEL)
    ycn, qkv, og, gates, tail = _premix_prompt(xp, sh1p, sc1p, w, nb, seq)
    hmn, c_p, n_p, m_p = _mlstm_prompt(qkv, og, gates, w["gml"], nb, seq)
    x1, h2t = _postmix(xp, ycn, hmn, gt1p, sc2p, sh2p, w["wo"], w["gffn"], seq)
    y_prompt = _peer_and_final(x1, h2t, gt2p, w, seq).reshape(nb, seq, D_MODEL)
    conv_p = tail[:, 6:8][None]
    c_p = c_p[None]
    n_p = n_p[:, :, 0][None]
    m_p = m_p[:, :, 0, 0][None]

    xs = x_sample.reshape(ts, D_MODEL)
    st = state_conv[0]
    sm1 = jnp.pad(st[:, 1:2], ((0, 0), (0, sseq - 1), (0, 0))).reshape(ts, D_CONV)
    sm2 = jnp.pad(st, ((0, 0), (0, sseq - 2), (0, 0))).reshape(ts, D_CONV)
    ycn, qkv, og, gates, u_s = _premix_sample(xs, sh1s, sc1s, sm1, sm2, w, sseq)

    def heads(a):
        return a.astype(F32).reshape(ns, sseq, HEADS, DH).transpose(0, 2, 1, 3).reshape(ns * HEADS, sseq, DH)

    q, k, v = (heads(qkv[:, i * D_MLSTM:(i + 1) * D_MLSTM]) for i in range(3))
    gate_t = gates.reshape(ns, sseq, 2, HEADS).transpose(2, 0, 3, 1).reshape(2, ns * HEADS, 1, sseq)
    gml_g = jnp.tile(g_mlstm[0].reshape(HEADS, 1, DH), (ns, 1, 1))
    hm, c_s, n_s, m_s = _mlstm_sample(
        q, q.transpose(0, 2, 1), k, k.transpose(0, 2, 1), v, heads(og), gate_t[0], gate_t[1], gml_g,
        state_mlstm_C[0].reshape(ns * HEADS, DH, DH), state_mlstm_n[0].reshape(ns * HEADS, 1, DH),
        state_mlstm_m[0].reshape(ns * HEADS, 1, 1), sseq)
    hmn = hm.reshape(ns, HEADS, sseq, DH).transpose(0, 2, 1, 3).reshape(ts, D_MLSTM).astype(BF16)
    per_tok = lambda a: a.reshape(ts // POST_ROWS, POST_ROWS, D_MODEL)
    x1, h2t = _postmix(xs, ycn, hmn, per_tok(gt1s), per_tok(sc2s), per_tok(sh2s), w["wo"], w["gffn"], POST_ROWS)
    y_sample = _peer_and_final(x1, h2t, per_tok(gt2s), w, POST_ROWS).reshape(ns, sseq, D_MODEL)
    conv_s = u_s.reshape(ns, sseq, D_CONV)[:, sseq - 2:][None]
    c_s = c_s.reshape(ns, HEADS, DH, DH)[None]
    n_s = n_s.reshape(ns, HEADS, DH)[None]
    m_s = m_s.reshape(ns, HEADS)[None]

    return (y_prompt, y_sample, conv_p, c_p, n_p, m_p, conv_s, c_s, n_s, m_s)
```

```python
import functools

import jax
import jax.numpy as jnp
import numpy as np
from jax import lax
from jax.experimental import pallas as pl
from jax.experimental.pallas import tpu as pltpu

F32 = jnp.float32
BF16 = jnp.bfloat16
NEG_INF = float("-inf")
EPS = 1e-6

D_MODEL = 1024
D_CONV = 512
CONV_GROUPS = 8
D_MLSTM = 512
HEADS = 4
DH = 128
D_MAIN = 3 * D_CONV + 4 * D_MLSTM
PEER_HEADS = 8
NKEYS = 128
TOPK = 16
NEXP = NKEYS * NKEYS

VMEM_LIMIT = 56 * 1024 * 1024

PREMIX_ROWS = 256
CHUNK = 256
POST_ROWS = 256
EXPERT_COLS = 512
EXPERT_ROWS = 1024
SAMPLE_GROUP = 32


def _dot(a, b):
    return jnp.dot(a, b, preferred_element_type=F32)


def _dot_nt(a, b):
    return lax.dot_general(a, b, (((1,), (1,)), ((), ())), preferred_element_type=F32)


def _dot_tn(a, b):
    return lax.dot_general(a, b, (((0,), (0,)), ((), ())), preferred_element_type=F32)


def _dot_f32(a, b):
    return jnp.dot(a, b, precision=lax.Precision.HIGHEST, preferred_element_type=F32)


def _params(*sem):
    return pltpu.CompilerParams(dimension_semantics=sem, vmem_limit_bytes=VMEM_LIMIT)


def _rms(x):
    return x * lax.rsqrt(jnp.mean(x * x, axis=-1, keepdims=True) + EPS)


def _mod_kernel(c_ref, w_ref, b_ref, o_ref):
    c = c_ref[...]
    a = (c * jax.nn.sigmoid(c)).astype(BF16)
    o_ref[...] = _dot(a, w_ref[...].astype(BF16)) + b_ref[...]


def _mod_rows(c, w_mod, b_mod):
    n = c.shape[0]
    tn = 1024
    return pl.pallas_call(
        _mod_kernel,
        grid=(w_mod.shape[1] // tn,),
        in_specs=[pl.BlockSpec((n, D_MODEL), lambda j: (0, 0)),
                  pl.BlockSpec((D_MODEL, tn), lambda j: (0, j)),
                  pl.BlockSpec((1, tn), lambda j: (0, j))],
        out_specs=pl.BlockSpec((n, tn), lambda j: (0, j)),
        out_shape=jax.ShapeDtypeStruct((n, w_mod.shape[1]), F32),
        compiler_params=_params("arbitrary"),
        name="mod_rows",
    )(c, w_mod, b_mod.reshape(1, -1))


def _log_sigmoid(x):
    return jnp.minimum(x, 0.0) - jnp.log1p(jnp.exp(-jnp.abs(x)))


def _premix_core(x, sh, sc, gmix, win_ref, wg_ref, bif):
    h = _rms(x) * gmix * (1.0 + sc) + sh
    z = _dot(h.astype(BF16), win_ref[...])
    pre = _dot_f32(h, wg_ref[...]) + bif
    lane = lax.broadcasted_iota(jnp.int32, pre.shape, 1)
    gates = jnp.where(lane < HEADS, pre, _log_sigmoid(pre))
    return z, gates


def _conv_tail(z, um1, um2, u, wconv, gconv, gmat_ref, ycn_ref, qkv_ref, og_ref):
    bg = z[:, D_CONV:2 * D_CONV]
    conv = wconv[0:1] * um2 + wconv[1:2] * um1 + wconv[2:3] * u
    yc = bg * conv
    ysq = yc * yc
    hi = ysq.astype(BF16)
    lo = (ysq - hi.astype(F32)).astype(BF16)
    gsum = _dot(hi, gmat_ref[...]) + _dot(lo, gmat_ref[...])
    ycn_ref[...] = (yc * lax.rsqrt(gsum * (CONV_GROUPS / D_CONV) + EPS) * gconv).astype(BF16)
    o0 = 3 * D_CONV
    qkv_ref[:, 0:D_MLSTM] = z[:, o0:o0 + D_MLSTM].astype(BF16)
    qkv_ref[:, D_MLSTM:2 * D_MLSTM] = (z[:, o0 + D_MLSTM:o0 + 2 * D_MLSTM] * (DH ** -0.5)).astype(BF16)
    qkv_ref[:, 2 * D_MLSTM:3 * D_MLSTM] = z[:, o0 + 2 * D_MLSTM:o0 + 3 * D_MLSTM].astype(BF16)
    og_ref[...] = jax.nn.sigmoid(z[:, o0 + 3 * D_MLSTM:o0 + 4 * D_MLSTM]).astype(BF16)


def _premix_prompt_kernel(x_ref, sh_ref, sc_ref, gmix_ref, win_ref, wg_ref, wconv_ref, bif_ref, gconv_ref,
                          gmat_ref, ycn_ref, qkv_ref, og_ref, gate_ref, tail_ref, carry_ref):
    rows = x_ref.shape[0]

    @pl.when(pl.program_id(1) == 0)
    def _():
        carry_ref[...] = jnp.zeros_like(carry_ref)

    z, gates = _premix_core(x_ref[...], sh_ref[0], sc_ref[0], gmix_ref[...], win_ref, wg_ref, bif_ref[...])
    gate_ref[...] = gates
    u = z[:, 2 * D_CONV:3 * D_CONV] * z[:, 0:D_CONV]
    prev = carry_ref[...]
    p0, p1 = prev[6:7], prev[7:8]
    ri = lax.broadcasted_iota(jnp.int32, (rows, 1), 0)
    um1 = jnp.where(ri == 0, p1, pltpu.roll(u, 1, 0))
    um2 = jnp.where(ri == 0, p0, jnp.where(ri == 1, p1, pltpu.roll(u, 2, 0)))
    carry_ref[...] = u[rows - 8:rows]
    tail_ref[0] = u[rows - 8:rows]
    _conv_tail(z, um1, um2, u, wconv_ref[...], gconv_ref[...], gmat_ref, ycn_ref, qkv_ref, og_ref)


def _premix_sample_kernel(x_ref, sh_ref, sc_ref, gmix_ref, win_ref, wg_ref, wconv_ref, bif_ref, gconv_ref,
                          gmat_ref, sm1_ref, sm2_ref, ycn_ref, qkv_ref, og_ref, gate_ref, u_ref, *, seq):
    rows = x_ref.shape[0]
    z, gates = _premix_core(x_ref[...], sh_ref[...], sc_ref[...], gmix_ref[...], win_ref, wg_ref, bif_ref[...])
    gate_ref[...] = gates
    u = z[:, 2 * D_CONV:3 * D_CONV] * z[:, 0:D_CONV]
    u_ref[...] = u
    tmod = lax.broadcasted_iota(jnp.int32, (rows, 1), 0) % seq
    um1 = jnp.where(tmod == 0, sm1_ref[...], pltpu.roll(u, 1, 0))
    um2 = jnp.where(tmod < 2, sm2_ref[...], pltpu.roll(u, 2, 0))
    _conv_tail(z, um1, um2, u, wconv_ref[...], gconv_ref[...], gmat_ref, ycn_ref, qkv_ref, og_ref)


def _const_spec(shape):
    nd = len(shape)
    return pl.BlockSpec(shape, lambda *_: (0,) * nd)


def _premix_weights(w):
    return [w["gmix"], w["win"], w["wg"], w["wconv"], w["bif"], w["gconv"], w["gmat"]]


def _premix_weight_specs():
    return [_const_spec((1, D_MODEL)), _const_spec((D_MODEL, D_MAIN)), _const_spec((D_MODEL, 8)),
            _const_spec((3, D_CONV)), _const_spec((1, 8)), _const_spec((1, D_CONV)),
            _const_spec((D_CONV, D_CONV))]


def _premix_prompt(x, sh, sc, w, nb, seq):
    t = x.shape[0]
    rows = PREMIX_ROWS
    nl = seq // rows
    tok = lambda b, l: (b * nl + l, 0)
    per_seq = pl.BlockSpec((1, 1, D_MODEL), lambda b, l: (b, 0, 0))
    return pl.pallas_call(
        _premix_prompt_kernel,
        grid=(nb, nl),
        in_specs=[pl.BlockSpec((rows, D_MODEL), tok), per_seq, per_seq] + _premix_weight_specs(),
        out_specs=[pl.BlockSpec((rows, D_CONV), tok), pl.BlockSpec((rows, 3 * D_MLSTM), tok),
                   pl.BlockSpec((rows, D_MLSTM), tok), pl.BlockSpec((rows, 8), tok),
                   pl.BlockSpec((1, 8, D_CONV), lambda b, l: (b, 0, 0))],
        out_shape=[jax.ShapeDtypeStruct((t, D_CONV), BF16), jax.ShapeDtypeStruct((t, 3 * D_MLSTM), BF16),
                   jax.ShapeDtypeStruct((t, D_MLSTM), BF16), jax.ShapeDtypeStruct((t, 8), F32),
                   jax.ShapeDtypeStruct((nb, 8, D_CONV), F32)],
        scratch_shapes=[pltpu.VMEM((8, D_CONV), F32)],
        compiler_params=_params("arbitrary", "arbitrary"),
        name="premix_prompt",
    )(x, sh, sc, *_premix_weights(w))


def _premix_sample(x, sh, sc, sm1, sm2, w, seq):
    t = x.shape[0]
    full = lambda n: _const_spec((t, n))
    return pl.pallas_call(
        functools.partial(_premix_sample_kernel, seq=seq),
        grid=(1,),
        in_specs=[full(D_MODEL), full(D_MODEL), full(D_MODEL)] + _premix_weight_specs()
        + [full(D_CONV), full(D_CONV)],
        out_specs=[full(D_CONV), full(3 * D_MLSTM), full(D_MLSTM), full(8), full(D_CONV)],
        out_shape=[jax.ShapeDtypeStruct((t, D_CONV), BF16), jax.ShapeDtypeStruct((t, 3 * D_MLSTM), BF16),
                   jax.ShapeDtypeStruct((t, D_MLSTM), BF16), jax.ShapeDtypeStruct((t, 8), F32),
                   jax.ShapeDtypeStruct((t, D_CONV), F32)],
        compiler_params=_params("arbitrary"),
        name="premix_sample",
    )(x, sh, sc, *_premix_weights(w), sm1, sm2)


def _mlstm_chunk_kernel(qkv_ref, og_ref, gate_ref, gml_ref, h_ref, c_ref, n_ref, m_ref, c_scr, n_scr, m_scr):
    lc = qkv_ref.shape[0]

    @pl.when(pl.program_id(1) == 0)
    def _():
        c_scr[...] = jnp.zeros_like(c_scr)
        n_scr[...] = jnp.zeros_like(n_scr)
        m_scr[...] = jnp.zeros_like(m_scr)

    gates = gate_ref[...]
    eye = (lax.broadcasted_iota(jnp.int32, (8, 8), 0) == lax.broadcasted_iota(jnp.int32, (8, 8), 1)).astype(F32)
    grow = lax.dot_general(eye, gates, (((1,), (1,)), ((), ())), precision=lax.Precision.HIGHEST,
                           preferred_element_type=F32)
    r = lax.broadcasted_iota(jnp.int32, (lc, lc), 0)
    c = lax.broadcasted_iota(jnp.int32, (lc, lc), 1)
    causal = c <= r
    tril = causal.astype(F32)
    bcol = _dot_f32(tril, gates)
    brow = lax.dot_general(grow, tril, (((1,), (1,)), ((), ())), precision=lax.Precision.HIGHEST,
                           preferred_element_type=F32)

    for h in range(HEADS):
        q = qkv_ref[:, h * DH:(h + 1) * DH]
        k = qkv_ref[:, D_MLSTM + h * DH:D_MLSTM + (h + 1) * DH]
        v = qkv_ref[:, 2 * D_MLSTM + h * DH:2 * D_MLSTM + (h + 1) * DH]
        li_c = gates[:, h:h + 1]
        li_r = grow[h:h + 1, :]
        b_c = bcol[:, HEADS + h:HEADS + h + 1]
        b_r = brow[HEADS + h:HEADS + h + 1, :]
        m_old = m_scr[h][0:1, 0:1]
        c_old = c_scr[h]
        n_old = n_scr[h][0:1, :]

        a = b_c + m_old
        dmat = jnp.where(causal, b_c - b_r + li_r, NEG_INF)
        mt = jnp.maximum(a, jnp.max(dmat, axis=1, keepdims=True))
        s = _dot_nt(q, k) * jnp.exp(dmat - mt)
        w_inter = jnp.exp(a - mt)
        num = w_inter * _dot(q, c_old.astype(BF16)) + _dot(s.astype(BF16), v)
        den = w_inter * jnp.sum(q.astype(F32) * n_old, axis=1, keepdims=True) + jnp.sum(s, axis=1, keepdims=True)
        hh = num / jnp.maximum(jnp.abs(den), jnp.exp(-mt))
        hh = og_ref[:, h * DH:(h + 1) * DH].astype(F32) * hh
        h_ref[:, h * DH:(h + 1) * DH] = (_rms(hh) * gml_ref[:, h * DH:(h + 1) * DH]).astype(BF16)

        bl = b_c[lc - 1:lc, :]
        gl = bl - b_c + li_c
        m_new = jnp.maximum(bl + m_old, jnp.max(gl, axis=0, keepdims=True))
        w_old = jnp.exp(bl + m_old - m_new)
        kw = k.astype(F32) * jnp.exp(gl - m_new)
        c_new = w_old * c_old + _dot_tn(kw.astype(BF16), v)
        n_new = w_old * n_old + jnp.sum(kw, axis=0, keepdims=True)
        c_scr[h] = c_new
        n_scr[h] = jnp.broadcast_to(n_new, (8, DH))
        m_scr[h] = jnp.broadcast_to(m_new, (8, DH))
        c_ref[0, h] = c_new
        n_ref[0, h] = jnp.broadcast_to(n_new, (8, DH))
        m_ref[0, h] = jnp.broadcast_to(m_new, (8, DH))


def _mlstm_prompt(qkv, og, gates, gml, nb, seq):
    t = qkv.shape[0]
    nc = seq // CHUNK
    tok = lambda b, c: (b * nc + c, 0)
    state = lambda b, c: (b, 0, 0, 0)
    return pl.pallas_call(
        _mlstm_chunk_kernel,
        grid=(nb, nc),
        in_specs=[pl.BlockSpec((CHUNK, 3 * D_MLSTM), tok), pl.BlockSpec((CHUNK, D_MLSTM), tok),
                  pl.BlockSpec((CHUNK, 8), tok), _const_spec((1, D_MLSTM))],
        out_specs=[pl.BlockSpec((CHUNK, D_MLSTM), tok), pl.BlockSpec((1, HEADS, DH, DH), state),
                   pl.BlockSpec((1, HEADS, 8, DH), state), pl.BlockSpec((1, HEADS, 8, DH), state)],
        out_shape=[jax.ShapeDtypeStruct((t, D_MLSTM), BF16), jax.ShapeDtypeStruct((nb, HEADS, DH, DH), F32),
                   jax.ShapeDtypeStruct((nb, HEADS, 8, DH), F32), jax.ShapeDtypeStruct((nb, HEADS, 8, DH), F32)],
        scratch_shapes=[pltpu.VMEM((HEADS, DH, DH), F32), pltpu.VMEM((HEADS, 8, DH), F32),
                        pltpu.VMEM((HEADS, 8, DH), F32)],
        compiler_params=_params("arbitrary", "arbitrary"),
        name="mlstm_prompt",
    )(qkv, og, gates, gml)


def _mlstm_step_kernel(q_ref, qt_ref, k_ref, kt_ref, v_ref, og_ref, li_ref, lf_ref, gml_ref, c0_ref, n0_ref,
                       m0_ref, h_ref, c_ref, n_ref, m_ref, *, seq):
    c = c0_ref[...]
    n = n0_ref[...]
    m = m0_ref[...]
    for t in range(seq):
        lit = li_ref[:, :, t:t + 1]
        lft = lf_ref[:, :, t:t + 1]
        m_new = jnp.maximum(lft + m, lit)
        fw = jnp.exp(lft + m - m_new)
        iw = jnp.exp(lit - m_new)
        c = fw * c + (iw * kt_ref[:, :, t:t + 1]) * v_ref[:, t:t + 1, :]
        n = fw * n + iw * k_ref[:, t:t + 1, :]
        num = jnp.sum(qt_ref[:, :, t:t + 1] * c, axis=1, keepdims=True)
        den = jnp.sum(q_ref[:, t:t + 1, :] * n, axis=2, keepdims=True)
        hh = og_ref[:, t:t + 1, :] * (num / jnp.maximum(jnp.abs(den), jnp.exp(-m_new)))
        h_ref[:, t:t + 1, :] = _rms(hh) * gml_ref[...]
        m = m_new
    c_ref[...] = c
    n_ref[...] = n
    m_ref[...] = m


def _mlstm_sample(q, qt, k, kt, v, og, li, lf, gml, c0, n0, m0, seq):
    ng = q.shape[0]
    g = SAMPLE_GROUP
    blk = lambda *shape: pl.BlockSpec((g,) + shape, lambda i: (i,) + (0,) * len(shape))
    return pl.pallas_call(
        functools.partial(_mlstm_step_kernel, seq=seq),
        grid=(ng // g,),
        in_specs=[blk(seq, DH), blk(DH, seq), blk(seq, DH), blk(DH, seq), blk(seq, DH), blk(seq, DH),
                  blk(1, seq), blk(1, seq), blk(1, DH), blk(DH, DH), blk(1, DH), blk(1, 1)],
        out_specs=[blk(seq, DH), blk(DH, DH), blk(1, DH), blk(1, 1)],
        out_shape=[jax.ShapeDtypeStruct((ng, seq, DH), F32), jax.ShapeDtypeStruct((ng, DH, DH), F32),
                   jax.ShapeDtypeStruct((ng, 1, DH), F32), jax.ShapeDtypeStruct((ng, 1, 1), F32)],
        compiler_params=_params("arbitrary"),
        name="mlstm_sample",
    )(q, qt, k, kt, v, og, li, lf, gml, c0, n0, m0)


def _postmix_kernel(x_ref, ycn_ref, hmn_ref, gt_ref, sc_ref, sh_ref, wo_ref, gffn_ref, x1_ref, h2t_ref):
    y = _dot(ycn_ref[...], wo_ref[0:D_CONV, :]) + _dot(hmn_ref[...], wo_ref[D_CONV:D_CONV + D_MLSTM, :])
    x1 = x_ref[...] + gt_ref[0] * y
    x1_ref[...] = x1
    h2 = _rms(x1) * gffn_ref[...] * (1.0 + sc_ref[0]) + sh_ref[0]
    h2t_ref[...] = h2.T.astype(BF16)


def _postmix(x, ycn, hmn, gt, sc, sh, wo, gffn, rows_per_mod):
    t = x.shape[0]
    rows = POST_ROWS
    r = gt.shape[1]
    per = rows_per_mod // rows
    tok = lambda i: (i, 0)
    mod = pl.BlockSpec((1, r, D_MODEL), lambda i: (i // per, 0, 0))
    return pl.pallas_call(
        _postmix_kernel,
        grid=(t // rows,),
        in_specs=[pl.BlockSpec((rows, D_MODEL), tok), pl.BlockSpec((rows, D_CONV), tok),
                  pl.BlockSpec((rows, D_MLSTM), tok), mod, mod, mod,
                  _const_spec((D_MODEL, D_MODEL)), _const_spec((1, D_MODEL))],
        out_specs=[pl.BlockSpec((rows, D_MODEL), tok), pl.BlockSpec((D_MODEL, rows), lambda i: (0, i))],
        out_shape=[jax.ShapeDtypeStruct((t, D_MODEL), F32), jax.ShapeDtypeStruct((D_MODEL, t), BF16)],
        compiler_params=_params("arbitrary"),
        name="postmix",
    )(x, ycn, hmn, gt, sc, sh, wo, gffn)


def _odd_even_merge_sort_pairs(n):
    pairs = []

    def merge(lo, m, r):
        step = r * 2
        if step < m:
            merge(lo, m, step)
            merge(lo + r, m, step)
            pairs.extend((i, i + r) for i in range(lo + r, lo + m - r, step))
        else:
            pairs.append((lo, lo + r))

    def sort(lo, m):
        if m > 1:
            sort(lo, m // 2)
            sort(lo + m // 2, m // 2)
            merge(lo, m, 1)

    sort(0, n)
    return pairs


_SORT16 = _odd_even_merge_sort_pairs(TOPK)
_BITONIC16 = [(i, i + d) for d in (8, 4, 2, 1) for i in range(TOPK) if not i & d]
SUBLANES = 8


def _exchange(x, pairs):
    for i, j in pairs:
        x[i], x[j] = jnp.maximum(x[i], x[j]), jnp.minimum(x[i], x[j])


def _top16(blocks):
    x = list(blocks)
    _exchange(x, _SORT16)
    for d in (4, 2, 1):
        p = [pltpu.roll(b, d, 0) for b in x]
        x = [jnp.maximum(x[r], p[TOPK - 1 - r]) for r in range(TOPK)]
        _exchange(x, _BITONIC16)
    return x


def _row_total(x):
    for d in (4, 2, 1):
        x = x + pltpu.roll(x, d, 0)
    return x


def _route_head(s1, s2):
    cols = s1.shape[1]
    nb = NKEYS // SUBLANES
    b1 = [s1[SUBLANES * r:SUBLANES * (r + 1)] for r in range(nb)]
    b2 = [s2[SUBLANES * r:SUBLANES * (r + 1)] for r in range(nb)]
    t1 = _top16(b1)
    t2 = _top16(b2)
    sub = lax.broadcasted_iota(jnp.int32, (SUBLANES, cols), 0)
    t2_lo, t2_hi, t1_hi = t2[0], t2[8], t1[8]
    for b in range(1, SUBLANES):
        t2_lo = jnp.where(sub == b, t2[b], t2_lo)
        t2_hi = jnp.where(sub == b, t2[8 + b], t2_hi)
        t1_hi = jnp.where(sub == b, t1[8 + b], t1_hi)
    stair = [t1[0] + t2_lo, t1[0] + t2_hi]
    stair += [jnp.where(sub < TOPK // (a + 1), t1[a] + t2_lo, NEG_INF) for a in range(1, 8)]
    stair += [t1_hi + t2[0]]
    tau = _top16(stair + [jnp.full((SUBLANES, cols), NEG_INF, F32)] * (TOPK - len(stair)))[TOPK - 1]
    top = t1[0] + t2[0]
    ones = [jnp.where(cand >= tau, 1.0, 0.0) for cand in stair]
    z = sum(jnp.where(cand >= tau, jnp.exp(cand - top), 0.0) for cand in stair)
    inv_z = 1.0 / _row_total(z)
    cnt = [_row_total(ones[0] + ones[1])] + [_row_total(o) for o in ones[2:9]]
    cnt += [jnp.where(t1[a] + t2[0] >= tau, 1.0, 0.0) for a in range(8, TOPK)]
    cnt1, c1, rank2, e2 = [], [], [], []
    for r in range(nb):
        c, k = jnp.zeros((SUBLANES, cols), F32), jnp.full((SUBLANES, cols), float(TOPK), F32)
        for a in reversed(range(TOPK)):
            c = jnp.where(b1[r] == t1[a], cnt[a], c)
            k = jnp.where(b2[r] == t2[a], float(a), k)
        cnt1.append(c)
        rank2.append(k)
        c1.append(jnp.exp(b1[r] - t1[0]) * (0.5 * inv_z))
        e2.append(jnp.exp(b2[r] - t2[0]))
    return cnt1, c1, rank2, e2


def _gelu_tanh_x2(a):
    c = 0.7978845608028654
    t = jnp.tanh(a * (c + (c * 0.044715) * (a * a)))
    return a + a * t


PACK = 16
LANES = 128


PEER_STEPS = NEXP // (2 * EXPERT_ROWS)


def _peer_kernel(h2t_ref, wpq_ref, keys_ref, u_ref, vta_ref, vtb_ref, o_ref,
                 cnt1_ref, c1_ref, rank2_ref, e2_ref, acc_ref, ga_ref, gb_ref):
    s = pl.program_id(1)
    ht = h2t_ref[...]
    cols = ht.shape[1]

    @pl.when(s == 0)
    def _():
        acc_ref[...] = jnp.zeros_like(acc_ref)
        gb_ref[...] = jnp.zeros_like(gb_ref)

        def head(h, carry):
            row0 = pl.multiple_of(h * 2 * NKEYS, 2 * NKEYS)
            qt = _dot(wpq_ref[pl.ds(row0, 2 * NKEYS), :], ht).astype(BF16)
            s1 = _dot(keys_ref[h], qt[0:NKEYS])
            s2 = _dot(keys_ref[PEER_HEADS + h], qt[NKEYS:2 * NKEYS])
            cnt1, c1, rank2, e2 = _route_head(s1, s2)
            for r in range(NKEYS // SUBLANES):
                cnt1_ref[h, SUBLANES * r:SUBLANES * (r + 1), :] = cnt1[r]
                c1_ref[h, SUBLANES * r:SUBLANES * (r + 1), :] = c1[r]
            for k in range(NKEYS // PACK):
                rank2_ref[h, k] = jnp.concatenate(rank2[2 * k:2 * k + 2], axis=0).astype(BF16)
                e2_ref[h, k] = jnp.concatenate(e2[2 * k:2 * k + 2], axis=0).astype(BF16)
            return carry

        lax.fori_loop(0, PEER_HEADS, head, 0)

    n_i1 = EXPERT_ROWS // NKEYS
    zero = jnp.zeros((), BF16)

    def build(tile, half, g_ref):
        row0 = pl.multiple_of(tile * n_i1, n_i1)
        for j in range(n_i1):
            u0 = half * EXPERT_ROWS + j * NKEYS
            a = _dot(u_ref[u0:u0 + NKEYS, :], ht)
            for cc in range(cols // LANES):
                lanes = slice(cc * LANES, (cc + 1) * LANES)
                w = [None] * (NKEYS // PACK)
                for h in range(PEER_HEADS):
                    cnt_row = cnt1_ref.at[h, pl.ds(row0, n_i1), :][j:j + 1, lanes]
                    c1_row = c1_ref.at[h, pl.ds(row0, n_i1), :][j:j + 1, lanes]
                    cnt = jnp.broadcast_to(cnt_row, (PACK, LANES)).astype(BF16)
                    c1 = jnp.broadcast_to(c1_row, (PACK, LANES)).astype(BF16)
                    for k in range(NKEYS // PACK):
                        sel = rank2_ref[h, k, :, lanes] < cnt
                        term = jnp.where(sel, e2_ref[h, k, :, lanes], zero) * c1
                        w[k] = term if h == 0 else w[k] + term
                for k in range(NKEYS // PACK):
                    rows = slice(k * PACK, (k + 1) * PACK)
                    g_ref[j * NKEYS + k * PACK:j * NKEYS + (k + 1) * PACK, lanes] = (
                        w[k] * _gelu_tanh_x2(a[rows, lanes]).astype(BF16))

    @pl.when(s < PEER_STEPS)
    def _():
        build(2 * s, 0, ga_ref)
        acc_ref[...] += _dot(vta_ref[...], gb_ref[...])
        build(2 * s + 1, 1, gb_ref)
        acc_ref[...] += _dot(vtb_ref[...], ga_ref[...])

    @pl.when(s == PEER_STEPS)
    def _():
        o_ref[...] = (acc_ref[...] + _dot(vta_ref[...], gb_ref[...])).T


def _peer(h2t, wpq, keys, u, vt):
    t = h2t.shape[1]
    cols = EXPERT_COLS
    last = NEXP // EXPERT_ROWS - 1
    return pl.pallas_call(
        _peer_kernel,
        grid=(t // cols, PEER_STEPS + 1),
        in_specs=[pl.BlockSpec((D_MODEL, cols), lambda i, s: (0, i)),
                  _const_spec((PEER_HEADS * 2 * NKEYS, D_MODEL)),
                  _const_spec((2 * PEER_HEADS, NKEYS, NKEYS)),
                  pl.BlockSpec((2 * EXPERT_ROWS, D_MODEL), lambda i, s: (jnp.minimum(s, PEER_STEPS - 1), 0)),
                  pl.BlockSpec((D_MODEL, EXPERT_ROWS), lambda i, s: (0, jnp.maximum(2 * s - 1, 0))),
                  pl.BlockSpec((D_MODEL, EXPERT_ROWS), lambda i, s: (0, jnp.minimum(2 * s, last)))],
        out_specs=pl.BlockSpec((cols, D_MODEL), lambda i, s: (i, 0)),
        out_shape=jax.ShapeDtypeStruct((t, D_MODEL), F32),
        scratch_shapes=[pltpu.VMEM((PEER_HEADS, NKEYS, cols), F32), pltpu.VMEM((PEER_HEADS, NKEYS, cols), F32),
                        pltpu.VMEM((PEER_HEADS, NKEYS // PACK, PACK, cols), BF16),
                        pltpu.VMEM((PEER_HEADS, NKEYS // PACK, PACK, cols), BF16),
                        pltpu.VMEM((D_MODEL, cols), F32),
                        pltpu.VMEM((EXPERT_ROWS, cols), BF16), pltpu.VMEM((EXPERT_ROWS, cols), BF16)],
        compiler_params=_params("arbitrary", "arbitrary"),
        name="peer",
    )(h2t, wpq, keys, u, vt, vt)


def _final_kernel(x1_ref, p_ref, gt_ref, gfin_ref, y_ref):
    y_ref[...] = _rms(x1_ref[...] + gt_ref[0] * p_ref[...]) * gfin_ref[...]


def _final(x1, p, gt, gfin, rows_per_mod):
    t = x1.shape[0]
    rows = POST_ROWS
    r = gt.shape[1]
    per = rows_per_mod // rows
    tok = pl.BlockSpec((rows, D_MODEL), lambda i: (i, 0))
    return pl.pallas_call(
        _final_kernel,
        grid=(t // rows,),
        in_specs=[tok, tok, pl.BlockSpec((1, r, D_MODEL), lambda i: (i // per, 0, 0)), _const_spec((1, D_MODEL))],
        out_specs=tok,
        out_shape=jax.ShapeDtypeStruct((t, D_MODEL), F32),
        compiler_params=_params("arbitrary"),
        name="final_norm",
    )(x1, p, gt, gfin)


def _group_matrix():
    g = np.arange(D_CONV) // (D_CONV // CONV_GROUPS)
    return jnp.asarray(g[:, None] == g[None, :], dtype=BF16)


def _peer_and_final(x1, h2t, gt2, w, rows_per_mod):
    p = _peer(h2t, w["wpq"], w["keys"], w["u"], w["vt"])
    return _final(x1, p, gt2, w["gfin"], rows_per_mod)


def kernel(x_prompt, x_sample, c_prompt, c_sample, state_conv, state_mlstm_C, state_mlstm_n, state_mlstm_m,
           w_mod, b_mod, g_mix, w_in, w_conv, b_i, b_f, g_conv, g_mlstm, w_out, g_ffn, w_pq, sub_keys,
           u_tab, v_tab, g_final):
    depth = w_mod.shape[0]
    assert depth == 1
    nb, seq, _ = x_prompt.shape
    ns, sseq, _ = x_sample.shape
    ts = ns * sseq

    w = {
        "gmix": g_mix[0].reshape(1, -1),
        "win": w_in[0][:, :D_MAIN].astype(BF16),
        "wg": w_in[0][:, D_MAIN:],
        "wconv": w_conv[0],
        "bif": jnp.concatenate([b_i[0], b_f[0]]).reshape(1, 8),
        "gconv": g_conv[0].reshape(1, -1),
        "gmat": _group_matrix(),
        "gml": g_mlstm[0].reshape(1, -1),
        "wo": w_out[0].astype(BF16),
        "gffn": g_ffn[0].reshape(1, -1),
        "wpq": w_pq[0].T.astype(BF16),
        "keys": sub_keys[0].reshape(2 * PEER_HEADS, NKEYS, -1).astype(BF16),
        "u": u_tab[0].astype(BF16),
        "vt": v_tab[0].T.astype(BF16),
        "gfin": g_final.reshape(1, -1),
    }

    mod = _mod_rows(jnp.concatenate([c_prompt, c_sample], axis=0), w_mod[0], b_mod[0])
    mod_p = mod[:nb].reshape(nb, 1, 6, D_MODEL)
    sh1p, sc1p, gt1p, sh2p, sc2p, gt2p = (mod_p[:, :, i] for i in range(6))
    mod_s = jnp.broadcast_to(mod[nb:].reshape(ns, 1, 6, D_MODEL), (ns, sseq, 6, D_MODEL)).reshape(ts, 6, D_MODEL)
    sh1s, sc1s, gt1s, sh2s, sc2s, gt2s = (mod_s[:, i] for i in range(6))

    xp = x_prompt.reshape(nb * seq, D_MODEL)
    ycn, qkv, og, gates, tail = _premix_prompt(xp, sh1p, sc1p, w, nb, seq)
    hmn, c_p, n_p, m_p = _mlstm_prompt(qkv, og, gates, w["gml"], nb, seq)
    x1, h2t = _postmix(xp, ycn, hmn, gt1p, sc2p, sh2p, w["wo"], w["gffn"], seq)
    y_prompt = _peer_and_final(x1, h2t, gt2p, w, seq).reshape(nb, seq, D_MODEL)
    conv_p = tail[:, 6:8][None]
    c_p = c_p[None]
    n_p = n_p[:, :, 0][None]
    m_p = m_p[:, :, 0, 0][None]

    xs = x_sample.reshape(ts, D_MODEL)
    st = state_conv[0]
    sm1 = jnp.pad(st[:, 1:2], ((0, 0), (0, sseq - 1), (0, 0))).reshape(ts, D_CONV)
    sm2 = jnp.pad(st, ((0, 0), (0, sseq - 2), (0, 0))).reshape(ts, D_CONV)
    ycn, qkv, og, gates, u_s = _premix_sample(xs, sh1s, sc1s, sm1, sm2, w, sseq)

    def heads(a):
        return a.astype(F32).reshape(ns, sseq, HEADS, DH).transpose(0, 2, 1, 3).reshape(ns * HEADS, sseq, DH)

    q, k, v = (heads(qkv[:, i * D_MLSTM:(i + 1) * D_MLSTM]) for i in range(3))
    gate_t = gates.reshape(ns, sseq, 2, HEADS).transpose(2, 0, 3, 1).reshape(2, ns * HEADS, 1, sseq)
    gml_g = jnp.tile(g_mlstm[0].reshape(HEADS, 1, DH), (ns, 1, 1))
    hm, c_s, n_s, m_s = _mlstm_sample(
        q, q.transpose(0, 2, 1), k, k.transpose(0, 2, 1), v, heads(og), gate_t[0], gate_t[1], gml_g,
        state_mlstm_C[0].reshape(ns * HEADS, DH, DH), state_mlstm_n[0].reshape(ns * HEADS, 1, DH),
        state_mlstm_m[0].reshape(ns * HEADS, 1, 1), sseq)
    hmn = hm.reshape(ns, HEADS, sseq, DH).transpose(0, 2, 1, 3).reshape(ts, D_MLSTM).astype(BF16)
    per_tok = lambda a: a.reshape(ts // POST_ROWS, POST_ROWS, D_MODEL)
    x1, h2t = _postmix(xs, ycn, hmn, per_tok(gt1s), per_tok(sc2s), per_tok(sh2s), w["wo"], w["gffn"], POST_ROWS)
    y_sample = _peer_and_final(x1, h2t, per_tok(gt2s), w, POST_ROWS).reshape(ns, sseq, D_MODEL)
    conv_s = u_s.reshape(ns, sseq, D_CONV)[:, sseq - 2:][None]
    c_s = c_s.reshape(ns, HEADS, DH, DH)[None]
    n_s = n_s.reshape(ns, HEADS, DH)[None]
    m_s = m_s.reshape(ns, HEADS)[None]

    return (y_prompt, y_sample, conv_p, c_p, n_p, m_p, conv_s, c_s, n_s, m_s)
```

```python
import functools

import jax
import jax.numpy as jnp
import numpy as np
from jax import lax
from jax.experimental import pallas as pl
from jax.experimental.pallas import tpu as pltpu

F32 = jnp.float32
BF16 = jnp.bfloat16
NEG_INF = float("-inf")
EPS = 1e-6

D_MODEL = 1024
D_CONV = 512
CONV_GROUPS = 8
D_MLSTM = 512
HEADS = 4
DH = 128
D_MAIN = 3 * D_CONV + 4 * D_MLSTM
PEER_HEADS = 8
NKEYS = 128
TOPK = 16
NEXP = NKEYS * NKEYS

VMEM_LIMIT = 56 * 1024 * 1024

PREMIX_ROWS = 256
CHUNK = 256
POST_ROWS = 512
EXPERT_COLS = 512
EXPERT_ROWS = 1024
SAMPLE_GROUP = 32


def _dot(a, b):
    return jnp.dot(a, b, preferred_element_type=F32)


def _dot_nt(a, b):
    return lax.dot_general(a, b, (((1,), (1,)), ((), ())), preferred_element_type=F32)


def _dot_tn(a, b):
    return lax.dot_general(a, b, (((0,), (0,)), ((), ())), preferred_element_type=F32)


def _dot_f32(a, b):
    return jnp.dot(a, b, precision=lax.Precision.HIGHEST, preferred_element_type=F32)


def _params(*sem):
    return pltpu.CompilerParams(dimension_semantics=sem, vmem_limit_bytes=VMEM_LIMIT)


def _rms(x):
    return x * lax.rsqrt(jnp.mean(x * x, axis=-1, keepdims=True) + EPS)


def _mod_kernel(c_ref, w_ref, b_ref, o_ref):
    c = c_ref[...]
    a = (c * jax.nn.sigmoid(c)).astype(BF16)
    o_ref[...] = _dot(a, w_ref[...].astype(BF16)) + b_ref[...]


def _mod_rows(c, w_mod, b_mod):
    n = c.shape[0]
    tn = 1024
    return pl.pallas_call(
        _mod_kernel,
        grid=(w_mod.shape[1] // tn,),
        in_specs=[pl.BlockSpec((n, D_MODEL), lambda j: (0, 0)),
                  pl.BlockSpec((D_MODEL, tn), lambda j: (0, j)),
                  pl.BlockSpec((1, tn), lambda j: (0, j))],
        out_specs=pl.BlockSpec((n, tn), lambda j: (0, j)),
        out_shape=jax.ShapeDtypeStruct((n, w_mod.shape[1]), F32),
        compiler_params=_params("arbitrary"),
        name="mod_rows",
    )(c, w_mod, b_mod.reshape(1, -1))


def _log_sigmoid(x):
    return jnp.minimum(x, 0.0) - jnp.log1p(jnp.exp(-jnp.abs(x)))


def _premix_core(x, sh, sc, gmix, win_ref, wg_ref, bif):
    h = _rms(x) * gmix * (1.0 + sc) + sh
    z = _dot(h.astype(BF16), win_ref[...])
    pre = _dot_f32(h, wg_ref[...]) + bif
    lane = lax.broadcasted_iota(jnp.int32, pre.shape, 1)
    gates = jnp.where(lane < HEADS, pre, _log_sigmoid(pre))
    return z, gates


def _conv_tail(z, um1, um2, u, wconv, gconv, gmat_ref, ycn_ref, qkv_ref, og_ref):
    bg = z[:, D_CONV:2 * D_CONV]
    conv = wconv[0:1] * um2 + wconv[1:2] * um1 + wconv[2:3] * u
    yc = bg * conv
    ysq = yc * yc
    hi = ysq.astype(BF16)
    lo = (ysq - hi.astype(F32)).astype(BF16)
    gsum = _dot(hi, gmat_ref[...]) + _dot(lo, gmat_ref[...])
    ycn_ref[...] = (yc * lax.rsqrt(gsum * (CONV_GROUPS / D_CONV) + EPS) * gconv).astype(BF16)
    o0 = 3 * D_CONV
    qkv_ref[:, 0:D_MLSTM] = z[:, o0:o0 + D_MLSTM].astype(BF16)
    qkv_ref[:, D_MLSTM:2 * D_MLSTM] = (z[:, o0 + D_MLSTM:o0 + 2 * D_MLSTM] * (DH ** -0.5)).astype(BF16)
    qkv_ref[:, 2 * D_MLSTM:3 * D_MLSTM] = z[:, o0 + 2 * D_MLSTM:o0 + 3 * D_MLSTM].astype(BF16)
    og_ref[...] = jax.nn.sigmoid(z[:, o0 + 3 * D_MLSTM:o0 + 4 * D_MLSTM]).astype(BF16)


def _premix_prompt_kernel(x_ref, sh_ref, sc_ref, gmix_ref, win_ref, wg_ref, wconv_ref, bif_ref, gconv_ref,
                          gmat_ref, ycn_ref, qkv_ref, og_ref, gate_ref, tail_ref, carry_ref):
    rows = x_ref.shape[0]

    @pl.when(pl.program_id(1) == 0)
    def _():
        carry_ref[...] = jnp.zeros_like(carry_ref)

    z, gates = _premix_core(x_ref[...], sh_ref[0], sc_ref[0], gmix_ref[...], win_ref, wg_ref, bif_ref[...])
    gate_ref[...] = gates
    u = z[:, 2 * D_CONV:3 * D_CONV] * z[:, 0:D_CONV]
    prev = carry_ref[...]
    p0, p1 = prev[6:7], prev[7:8]
    ri = lax.broadcasted_iota(jnp.int32, (rows, 1), 0)
    um1 = jnp.where(ri == 0, p1, pltpu.roll(u, 1, 0))
    um2 = jnp.where(ri == 0, p0, jnp.where(ri == 1, p1, pltpu.roll(u, 2, 0)))
    carry_ref[...] = u[rows - 8:rows]
    tail_ref[0] = u[rows - 8:rows]
    _conv_tail(z, um1, um2, u, wconv_ref[...], gconv_ref[...], gmat_ref, ycn_ref, qkv_ref, og_ref)


def _premix_sample_kernel(x_ref, sh_ref, sc_ref, gmix_ref, win_ref, wg_ref, wconv_ref, bif_ref, gconv_ref,
                          gmat_ref, sm1_ref, sm2_ref, ycn_ref, qkv_ref, og_ref, gate_ref, u_ref, *, seq):
    rows = x_ref.shape[0]
    z, gates = _premix_core(x_ref[...], sh_ref[...], sc_ref[...], gmix_ref[...], win_ref, wg_ref, bif_ref[...])
    gate_ref[...] = gates
    u = z[:, 2 * D_CONV:3 * D_CONV] * z[:, 0:D_CONV]
    u_ref[...] = u
    tmod = lax.broadcasted_iota(jnp.int32, (rows, 1), 0) % seq
    um1 = jnp.where(tmod == 0, sm1_ref[...], pltpu.roll(u, 1, 0))
    um2 = jnp.where(tmod < 2, sm2_ref[...], pltpu.roll(u, 2, 0))
    _conv_tail(z, um1, um2, u, wconv_ref[...], gconv_ref[...], gmat_ref, ycn_ref, qkv_ref, og_ref)


def _const_spec(shape):
    nd = len(shape)
    return pl.BlockSpec(shape, lambda *_: (0,) * nd)


def _premix_weights(w):
    return [w["gmix"], w["win"], w["wg"], w["wconv"], w["bif"], w["gconv"], w["gmat"]]


def _premix_weight_specs():
    return [_const_spec((1, D_MODEL)), _const_spec((D_MODEL, D_MAIN)), _const_spec((D_MODEL, 8)),
            _const_spec((3, D_CONV)), _const_spec((1, 8)), _const_spec((1, D_CONV)),
            _const_spec((D_CONV, D_CONV))]


def _premix_prompt(x, sh, sc, w, nb, seq):
    t = x.shape[0]
    rows = PREMIX_ROWS
    nl = seq // rows
    tok = lambda b, l: (b * nl + l, 0)
    per_seq = pl.BlockSpec((1, 1, D_MODEL), lambda b, l: (b, 0, 0))
    return pl.pallas_call(
        _premix_prompt_kernel,
        grid=(nb, nl),
        in_specs=[pl.BlockSpec((rows, D_MODEL), tok), per_seq, per_seq] + _premix_weight_specs(),
        out_specs=[pl.BlockSpec((rows, D_CONV), tok), pl.BlockSpec((rows, 3 * D_MLSTM), tok),
                   pl.BlockSpec((rows, D_MLSTM), tok), pl.BlockSpec((rows, 8), tok),
                   pl.BlockSpec((1, 8, D_CONV), lambda b, l: (b, 0, 0))],
        out_shape=[jax.ShapeDtypeStruct((t, D_CONV), BF16), jax.ShapeDtypeStruct((t, 3 * D_MLSTM), BF16),
                   jax.ShapeDtypeStruct((t, D_MLSTM), BF16), jax.ShapeDtypeStruct((t, 8), F32),
                   jax.ShapeDtypeStruct((nb, 8, D_CONV), F32)],
        scratch_shapes=[pltpu.VMEM((8, D_CONV), F32)],
        compiler_params=_params("arbitrary", "arbitrary"),
        name="premix_prompt",
    )(x, sh, sc, *_premix_weights(w))


def _premix_sample(x, sh, sc, sm1, sm2, w, seq):
    t = x.shape[0]
    full = lambda n: _const_spec((t, n))
    return pl.pallas_call(
        functools.partial(_premix_sample_kernel, seq=seq),
        grid=(1,),
        in_specs=[full(D_MODEL), full(D_MODEL), full(D_MODEL)] + _premix_weight_specs()
        + [full(D_CONV), full(D_CONV)],
        out_specs=[full(D_CONV), full(3 * D_MLSTM), full(D_MLSTM), full(8), full(D_CONV)],
        out_shape=[jax.ShapeDtypeStruct((t, D_CONV), BF16), jax.ShapeDtypeStruct((t, 3 * D_MLSTM), BF16),
                   jax.ShapeDtypeStruct((t, D_MLSTM), BF16), jax.ShapeDtypeStruct((t, 8), F32),
                   jax.ShapeDtypeStruct((t, D_CONV), F32)],
        compiler_params=_params("arbitrary"),
        name="premix_sample",
    )(x, sh, sc, *_premix_weights(w), sm1, sm2)


def _mlstm_chunk_kernel(qkv_ref, og_ref, gate_ref, gml_ref, h_ref, c_ref, n_ref, m_ref, c_scr, n_scr, m_scr):
    lc = qkv_ref.shape[0]

    @pl.when(pl.program_id(1) == 0)
    def _():
        c_scr[...] = jnp.zeros_like(c_scr)
        n_scr[...] = jnp.zeros_like(n_scr)
        m_scr[...] = jnp.zeros_like(m_scr)

    gates = gate_ref[...]
    eye = (lax.broadcasted_iota(jnp.int32, (8, 8), 0) == lax.broadcasted_iota(jnp.int32, (8, 8), 1)).astype(F32)
    grow = lax.dot_general(eye, gates, (((1,), (1,)), ((), ())), precision=lax.Precision.HIGHEST,
                           preferred_element_type=F32)
    r = lax.broadcasted_iota(jnp.int32, (lc, lc), 0)
    c = lax.broadcasted_iota(jnp.int32, (lc, lc), 1)
    causal = c <= r
    tril = causal.astype(F32)
    bcol = _dot_f32(tril, gates)
    brow = lax.dot_general(grow, tril, (((1,), (1,)), ((), ())), precision=lax.Precision.HIGHEST,
                           preferred_element_type=F32)

    for h in range(HEADS):
        q = qkv_ref[:, h * DH:(h + 1) * DH]
        k = qkv_ref[:, D_MLSTM + h * DH:D_MLSTM + (h + 1) * DH]
        v = qkv_ref[:, 2 * D_MLSTM + h * DH:2 * D_MLSTM + (h + 1) * DH]
        li_c = gates[:, h:h + 1]
        li_r = grow[h:h + 1, :]
        b_c = bcol[:, HEADS + h:HEADS + h + 1]
        b_r = brow[HEADS + h:HEADS + h + 1, :]
        m_old = m_scr[h][0:1, 0:1]
        c_old = c_scr[h]
        n_old = n_scr[h][0:1, :]

        a = b_c + m_old
        dmat = jnp.where(causal, b_c - b_r + li_r, NEG_INF)
        mt = jnp.maximum(a, jnp.max(dmat, axis=1, keepdims=True))
        s = _dot_nt(q, k) * jnp.exp(dmat - mt)
        w_inter = jnp.exp(a - mt)
        num = w_inter * _dot(q, c_old.astype(BF16)) + _dot(s.astype(BF16), v)
        den = w_inter * jnp.sum(q.astype(F32) * n_old, axis=1, keepdims=True) + jnp.sum(s, axis=1, keepdims=True)
        hh = num / jnp.maximum(jnp.abs(den), jnp.exp(-mt))
        hh = og_ref[:, h * DH:(h + 1) * DH].astype(F32) * hh
        h_ref[:, h * DH:(h + 1) * DH] = (_rms(hh) * gml_ref[:, h * DH:(h + 1) * DH]).astype(BF16)

        bl = b_c[lc - 1:lc, :]
        gl = bl - b_c + li_c
        m_new = jnp.maximum(bl + m_old, jnp.max(gl, axis=0, keepdims=True))
        w_old = jnp.exp(bl + m_old - m_new)
        kw = k.astype(F32) * jnp.exp(gl - m_new)
        c_new = w_old * c_old + _dot_tn(kw.astype(BF16), v)
        n_new = w_old * n_old + jnp.sum(kw, axis=0, keepdims=True)
        c_scr[h] = c_new
        n_scr[h] = jnp.broadcast_to(n_new, (8, DH))
        m_scr[h] = jnp.broadcast_to(m_new, (8, DH))
        c_ref[0, h] = c_new
        n_ref[0, h] = jnp.broadcast_to(n_new, (8, DH))
        m_ref[0, h] = jnp.broadcast_to(m_new, (8, DH))


def _mlstm_prompt(qkv, og, gates, gml, nb, seq):
    t = qkv.shape[0]
    nc = seq // CHUNK
    tok = lambda b, c: (b * nc + c, 0)
    state = lambda b, c: (b, 0, 0, 0)
    return pl.pallas_call(
        _mlstm_chunk_kernel,
        grid=(nb, nc),
        in_specs=[pl.BlockSpec((CHUNK, 3 * D_MLSTM), tok), pl.BlockSpec((CHUNK, D_MLSTM), tok),
                  pl.BlockSpec((CHUNK, 8), tok), _const_spec((1, D_MLSTM))],
        out_specs=[pl.BlockSpec((CHUNK, D_MLSTM), tok), pl.BlockSpec((1, HEADS, DH, DH), state),
                   pl.BlockSpec((1, HEADS, 8, DH), state), pl.BlockSpec((1, HEADS, 8, DH), state)],
        out_shape=[jax.ShapeDtypeStruct((t, D_MLSTM), BF16), jax.ShapeDtypeStruct((nb, HEADS, DH, DH), F32),
                   jax.ShapeDtypeStruct((nb, HEADS, 8, DH), F32), jax.ShapeDtypeStruct((nb, HEADS, 8, DH), F32)],
        scratch_shapes=[pltpu.VMEM((HEADS, DH, DH), F32), pltpu.VMEM((HEADS, 8, DH), F32),
                        pltpu.VMEM((HEADS, 8, DH), F32)],
        compiler_params=_params("arbitrary", "arbitrary"),
        name="mlstm_prompt",
    )(qkv, og, gates, gml)


def _mlstm_step_kernel(q_ref, qt_ref, k_ref, kt_ref, v_ref, og_ref, li_ref, lf_ref, gml_ref, c0_ref, n0_ref,
                       m0_ref, h_ref, c_ref, n_ref, m_ref, *, seq):
    c = c0_ref[...]
    n = n0_ref[...]
    m = m0_ref[...]
    for t in range(seq):
        lit = li_ref[:, :, t:t + 1]
        lft = lf_ref[:, :, t:t + 1]
        m_new = jnp.maximum(lft + m, lit)
        fw = jnp.exp(lft + m - m_new)
        iw = jnp.exp(lit - m_new)
        c = fw * c + (iw * kt_ref[:, :, t:t + 1]) * v_ref[:, t:t + 1, :]
        n = fw * n + iw * k_ref[:, t:t + 1, :]
        num = jnp.sum(qt_ref[:, :, t:t + 1] * c, axis=1, keepdims=True)
        den = jnp.sum(q_ref[:, t:t + 1, :] * n, axis=2, keepdims=True)
        hh = og_ref[:, t:t + 1, :] * (num / jnp.maximum(jnp.abs(den), jnp.exp(-m_new)))
        h_ref[:, t:t + 1, :] = _rms(hh) * gml_ref[...]
        m = m_new
    c_ref[...] = c
    n_ref[...] = n
    m_ref[...] = m


def _mlstm_sample(q, qt, k, kt, v, og, li, lf, gml, c0, n0, m0, seq):
    ng = q.shape[0]
    g = SAMPLE_GROUP
    blk = lambda *shape: pl.BlockSpec((g,) + shape, lambda i: (i,) + (0,) * len(shape))
    return pl.pallas_call(
        functools.partial(_mlstm_step_kernel, seq=seq),
        grid=(ng // g,),
        in_specs=[blk(seq, DH), blk(DH, seq), blk(seq, DH), blk(DH, seq), blk(seq, DH), blk(seq, DH),
                  blk(1, seq), blk(1, seq), blk(1, DH), blk(DH, DH), blk(1, DH), blk(1, 1)],
        out_specs=[blk(seq, DH), blk(DH, DH), blk(1, DH), blk(1, 1)],
        out_shape=[jax.ShapeDtypeStruct((ng, seq, DH), F32), jax.ShapeDtypeStruct((ng, DH, DH), F32),
                   jax.ShapeDtypeStruct((ng, 1, DH), F32), jax.ShapeDtypeStruct((ng, 1, 1), F32)],
        compiler_params=_params("arbitrary"),
        name="mlstm_sample",
    )(q, qt, k, kt, v, og, li, lf, gml, c0, n0, m0)


def _postmix_kernel(x_ref, ycn_ref, hmn_ref, gt_ref, sc_ref, sh_ref, wo_ref, gffn_ref, *rest):
    x1_ref, h2t_ref = rest[-2:]
    y = _dot(ycn_ref[...], wo_ref[0:D_CONV, :]) + _dot(hmn_ref[...], wo_ref[D_CONV:D_CONV + D_MLSTM, :])
    x1 = x_ref[...] + gt_ref[0] * y
    x1_ref[...] = x1
    h2 = _rms(x1) * gffn_ref[...] * (1.0 + sc_ref[0]) + sh_ref[0]
    h2t_ref[...] = h2.T.astype(BF16)


def _postmix(x, ycn, hmn, gt, sc, sh, wo, gffn, rows_per_mod, total, col0, h2t_buf=None):
    t = x.shape[0]
    rows = POST_ROWS
    r = gt.shape[1]
    per = rows_per_mod // rows
    tok = lambda i: (i, 0)
    mod = pl.BlockSpec((1, r, D_MODEL), lambda i: (i // per, 0, 0))
    in_specs = [pl.BlockSpec((rows, D_MODEL), tok), pl.BlockSpec((rows, D_CONV), tok),
                pl.BlockSpec((rows, D_MLSTM), tok), mod, mod, mod,
                _const_spec((D_MODEL, D_MODEL)), _const_spec((1, D_MODEL))]
    args = [x, ycn, hmn, gt, sc, sh, wo, gffn]
    aliases = {}
    if h2t_buf is not None:
        in_specs.append(pl.BlockSpec(memory_space=pl.ANY))
        args.append(h2t_buf)
        aliases = {len(args) - 1: 1}
    return pl.pallas_call(
        _postmix_kernel,
        grid=(t // rows,),
        in_specs=in_specs,
        out_specs=[pl.BlockSpec((rows, D_MODEL), tok),
                   pl.BlockSpec((D_MODEL, rows), lambda i: (0, col0 // rows + i))],
        out_shape=[jax.ShapeDtypeStruct((t, D_MODEL), F32), jax.ShapeDtypeStruct((D_MODEL, total), BF16)],
        input_output_aliases=aliases,
        compiler_params=_params("arbitrary"),
        name="postmix",
    )(*args)


def _odd_even_merge_sort_pairs(n):
    pairs = []

    def merge(lo, m, r):
        step = r * 2
        if step < m:
            merge(lo, m, step)
            merge(lo + r, m, step)
            pairs.extend((i, i + r) for i in range(lo + r, lo + m - r, step))
        else:
            pairs.append((lo, lo + r))

    def sort(lo, m):
        if m > 1:
            sort(lo, m // 2)
            sort(lo + m // 2, m // 2)
            merge(lo, m, 1)

    sort(0, n)
    return pairs


_SORT16 = _odd_even_merge_sort_pairs(TOPK)
_BITONIC16 = [(i, i + d) for d in (8, 4, 2, 1) for i in range(TOPK) if not i & d]
SUBLANES = 8


def _exchange(x, pairs):
    for i, j in pairs:
        x[i], x[j] = jnp.maximum(x[i], x[j]), jnp.minimum(x[i], x[j])


def _top16(blocks):
    x = list(blocks)
    _exchange(x, _SORT16)
    for d in (4, 2, 1):
        p = [pltpu.roll(b, d, 0) for b in x]
        x = [jnp.maximum(x[r], p[TOPK - 1 - r]) for r in range(TOPK)]
        _exchange(x, _BITONIC16)
    return x


def _row_total(x):
    for d in (4, 2, 1):
        x = x + pltpu.roll(x, d, 0)
    return x


def _route_head(s1, s2):
    cols = s1.shape[1]
    nb = NKEYS // SUBLANES
    b1 = [s1[SUBLANES * r:SUBLANES * (r + 1)] for r in range(nb)]
    b2 = [s2[SUBLANES * r:SUBLANES * (r + 1)] for r in range(nb)]
    t1 = _top16(b1)
    t2 = _top16(b2)
    sub = lax.broadcasted_iota(jnp.int32, (SUBLANES, cols), 0)
    t2_lo, t2_hi, t1_hi = t2[0], t2[8], t1[8]
    for b in range(1, SUBLANES):
        t2_lo = jnp.where(sub == b, t2[b], t2_lo)
        t2_hi = jnp.where(sub == b, t2[8 + b], t2_hi)
        t1_hi = jnp.where(sub == b, t1[8 + b], t1_hi)
    stair = [t1[0] + t2_lo, t1[0] + t2_hi]
    stair += [jnp.where(sub < TOPK // (a + 1), t1[a] + t2_lo, NEG_INF) for a in range(1, 8)]
    stair += [t1_hi + t2[0]]
    tau = _top16(stair + [jnp.full((SUBLANES, cols), NEG_INF, F32)] * (TOPK - len(stair)))[TOPK - 1]
    top = t1[0] + t2[0]
    ones = [jnp.where(cand >= tau, 1.0, 0.0) for cand in stair]
    z = sum(jnp.where(cand >= tau, jnp.exp(cand - top), 0.0) for cand in stair)
    inv_z = 1.0 / _row_total(z)
    cnt = [_row_total(ones[0] + ones[1])] + [_row_total(o) for o in ones[2:9]]
    cnt += [jnp.where(t1[a] + t2[0] >= tau, 1.0, 0.0) for a in range(8, TOPK)]
    cnt1, c1, rank2, e2 = [], [], [], []
    for r in range(nb):
        c, k = jnp.zeros((SUBLANES, cols), F32), jnp.full((SUBLANES, cols), float(TOPK), F32)
        for a in reversed(range(TOPK)):
            c = jnp.where(b1[r] == t1[a], cnt[a], c)
            k = jnp.where(b2[r] == t2[a], float(a), k)
        cnt1.append(c)
        rank2.append(k)
        c1.append(jnp.exp(b1[r] - t1[0]) * (0.5 * inv_z))
        e2.append(jnp.exp(b2[r] - t2[0]))
    return cnt1, c1, rank2, e2


def _gelu_tanh_x2(a):
    c = 0.7978845608028654
    t = jnp.tanh(a * (c + (c * 0.044715) * (a * a)))
    return a + a * t


PACK = 16
LANES = 128


PEER_STEPS = NEXP // (2 * EXPERT_ROWS)


def _peer_kernel(h2t_ref, wpq_ref, keys_ref, u_ref, vta_ref, vtb_ref, o_ref,
                 cnt1_ref, c1_ref, rank2_ref, e2_ref, acc_ref, ga_ref, gb_ref):
    s = pl.program_id(1)
    ht = h2t_ref[...]
    cols = ht.shape[1]

    @pl.when(s == 0)
    def _():
        acc_ref[...] = jnp.zeros_like(acc_ref)
        gb_ref[...] = jnp.zeros_like(gb_ref)

        def head(h, carry):
            row0 = pl.multiple_of(h * 2 * NKEYS, 2 * NKEYS)
            qt = _dot(wpq_ref[pl.ds(row0, 2 * NKEYS), :], ht).astype(BF16)
            s1 = _dot(keys_ref[h], qt[0:NKEYS])
            s2 = _dot(keys_ref[PEER_HEADS + h], qt[NKEYS:2 * NKEYS])
            cnt1, c1, rank2, e2 = _route_head(s1, s2)
            for r in range(NKEYS // SUBLANES):
                cnt1_ref[h, SUBLANES * r:SUBLANES * (r + 1), :] = cnt1[r]
                c1_ref[h, SUBLANES * r:SUBLANES * (r + 1), :] = c1[r]
            for k in range(NKEYS // PACK):
                rank2_ref[h, k] = jnp.concatenate(rank2[2 * k:2 * k + 2], axis=0).astype(BF16)
                e2_ref[h, k] = jnp.concatenate(e2[2 * k:2 * k + 2], axis=0).astype(BF16)
            return carry

        lax.fori_loop(0, PEER_HEADS, head, 0)

    n_i1 = EXPERT_ROWS // NKEYS
    zero = jnp.zeros((), BF16)

    def build(tile, half, g_ref):
        row0 = pl.multiple_of(tile * n_i1, n_i1)
        for j in range(n_i1):
            u0 = half * EXPERT_ROWS + j * NKEYS
            a = _dot(u_ref[u0:u0 + NKEYS, :], ht)
            for cc in range(cols // LANES):
                lanes = slice(cc * LANES, (cc + 1) * LANES)
                w = [None] * (NKEYS // PACK)
                for h in range(PEER_HEADS):
                    cnt_row = cnt1_ref.at[h, pl.ds(row0, n_i1), :][j:j + 1, lanes]
                    c1_row = c1_ref.at[h, pl.ds(row0, n_i1), :][j:j + 1, lanes]
                    cnt = jnp.broadcast_to(cnt_row, (PACK, LANES)).astype(BF16)
                    c1 = jnp.broadcast_to(c1_row, (PACK, LANES)).astype(BF16)
                    for k in range(NKEYS // PACK):
                        sel = rank2_ref[h, k, :, lanes] < cnt
                        term = jnp.where(sel, e2_ref[h, k, :, lanes], zero) * c1
                        w[k] = term if h == 0 else w[k] + term
                for k in range(NKEYS // PACK):
                    rows = slice(k * PACK, (k + 1) * PACK)
                    g_ref[j * NKEYS + k * PACK:j * NKEYS + (k + 1) * PACK, lanes] = (
                        w[k] * _gelu_tanh_x2(a[rows, lanes]).astype(BF16))

    @pl.when(s < PEER_STEPS)
    def _():
        build(2 * s, 0, ga_ref)
        acc_ref[...] += _dot(vta_ref[0], gb_ref[...])
        build(2 * s + 1, 1, gb_ref)
        acc_ref[...] += _dot(vtb_ref[0], ga_ref[...])

    @pl.when(s == PEER_STEPS)
    def _():
        o_ref[...] = (acc_ref[...] + _dot(vta_ref[0], gb_ref[...])).T


def _peer(h2t, wpq, keys, u, vt):
    t = h2t.shape[1]
    cols = EXPERT_COLS
    last = NEXP // EXPERT_ROWS - 1
    return pl.pallas_call(
        _peer_kernel,
        grid=(t // cols, PEER_STEPS + 1),
        in_specs=[pl.BlockSpec((D_MODEL, cols), lambda i, s: (0, i)),
                  _const_spec((PEER_HEADS * 2 * NKEYS, D_MODEL)),
                  _const_spec((2 * PEER_HEADS, NKEYS, NKEYS)),
                  pl.BlockSpec((2 * EXPERT_ROWS, D_MODEL), lambda i, s: (jnp.minimum(s, PEER_STEPS - 1), 0)),
                  pl.BlockSpec((1, D_MODEL, EXPERT_ROWS), lambda i, s: (jnp.maximum(2 * s - 1, 0), 0, 0)),
                  pl.BlockSpec((1, D_MODEL, EXPERT_ROWS), lambda i, s: (jnp.minimum(2 * s, last), 0, 0))],
        out_specs=pl.BlockSpec((cols, D_MODEL), lambda i, s: (i, 0)),
        out_shape=jax.ShapeDtypeStruct((t, D_MODEL), F32),
        scratch_shapes=[pltpu.VMEM((PEER_HEADS, NKEYS, cols), F32), pltpu.VMEM((PEER_HEADS, NKEYS, cols), F32),
                        pltpu.VMEM((PEER_HEADS, NKEYS // PACK, PACK, cols), BF16),
                        pltpu.VMEM((PEER_HEADS, NKEYS // PACK, PACK, cols), BF16),
                        pltpu.VMEM((D_MODEL, cols), F32),
                        pltpu.VMEM((EXPERT_ROWS, cols), BF16), pltpu.VMEM((EXPERT_ROWS, cols), BF16)],
        compiler_params=_params("arbitrary", "arbitrary"),
        name="peer",
    )(h2t, wpq, keys, u, vt, vt)


def _final_kernel(x1_ref, p_ref, gt_ref, gfin_ref, y_ref):
    y_ref[...] = _rms(x1_ref[...] + gt_ref[0] * p_ref[...]) * gfin_ref[...]


def _final(x1, p, gt, gfin, rows_per_mod, row0):
    t = x1.shape[0]
    rows = POST_ROWS
    r = gt.shape[1]
    per = rows_per_mod // rows
    tok = pl.BlockSpec((rows, D_MODEL), lambda i: (i, 0))
    return pl.pallas_call(
        _final_kernel,
        grid=(t // rows,),
        in_specs=[tok, pl.BlockSpec((rows, D_MODEL), lambda i: (row0 // rows + i, 0)),
                  pl.BlockSpec((1, r, D_MODEL), lambda i: (i // per, 0, 0)), _const_spec((1, D_MODEL))],
        out_specs=tok,
        out_shape=jax.ShapeDtypeStruct((t, D_MODEL), F32),
        compiler_params=_params("arbitrary"),
        name="final_norm",
    )(x1, p, gt, gfin)


def _group_matrix():
    g = np.arange(D_CONV) // (D_CONV // CONV_GROUPS)
    return jnp.asarray(g[:, None] == g[None, :], dtype=BF16)


def kernel(x_prompt, x_sample, c_prompt, c_sample, state_conv, state_mlstm_C, state_mlstm_n, state_mlstm_m,
           w_mod, b_mod, g_mix, w_in, w_conv, b_i, b_f, g_conv, g_mlstm, w_out, g_ffn, w_pq, sub_keys,
           u_tab, v_tab, g_final):
    depth = w_mod.shape[0]
    assert depth == 1
    nb, seq, _ = x_prompt.shape
    ns, sseq, _ = x_sample.shape
    ts = ns * sseq

    w = {
        "gmix": g_mix[0].reshape(1, -1),
        "win": w_in[0][:, :D_MAIN].astype(BF16),
        "wg": w_in[0][:, D_MAIN:],
        "wconv": w_conv[0],
        "bif": jnp.concatenate([b_i[0], b_f[0]]).reshape(1, 8),
        "gconv": g_conv[0].reshape(1, -1),
        "gmat": _group_matrix(),
        "gml": g_mlstm[0].reshape(1, -1),
        "wo": w_out[0].astype(BF16),
        "gffn": g_ffn[0].reshape(1, -1),
        "wpq": w_pq[0].T.astype(BF16),
        "keys": sub_keys[0].reshape(2 * PEER_HEADS, NKEYS, -1).astype(BF16),
        "u": u_tab[0].astype(BF16),
        "vt": v_tab[0].reshape(NEXP // EXPERT_ROWS, EXPERT_ROWS, D_MODEL).transpose(0, 2, 1).astype(BF16),
        "gfin": g_final.reshape(1, -1),
    }

    mod = _mod_rows(jnp.concatenate([c_prompt, c_sample], axis=0), w_mod[0], b_mod[0])
    mod_p = mod[:nb].reshape(nb, 1, 6, D_MODEL)
    sh1p, sc1p, gt1p, sh2p, sc2p, gt2p = (mod_p[:, :, i] for i in range(6))
    mod_s = jnp.broadcast_to(mod[nb:].reshape(ns, 1, 6, D_MODEL), (ns, sseq, 6, D_MODEL)).reshape(ts, 6, D_MODEL)
    sh1s, sc1s, gt1s, sh2s, sc2s, gt2s = (mod_s[:, i] for i in range(6))

    xp = x_prompt.reshape(nb * seq, D_MODEL)
    ycn, qkv, og, gates, tail = _premix_prompt(xp, sh1p, sc1p, w, nb, seq)
    hmn, c_p, n_p, m_p = _mlstm_prompt(qkv, og, gates, w["gml"], nb, seq)
    tp = nb * seq
    x1p, h2t = _postmix(xp, ycn, hmn, gt1p, sc2p, sh2p, w["wo"], w["gffn"], seq, tp + ts, 0)
    conv_p = tail[:, 6:8][None]
    c_p = c_p[None]
    n_p = n_p[:, :, 0][None]
    m_p = m_p[:, :, 0, 0][None]

    xs = x_sample.reshape(ts, D_MODEL)
    st = state_conv[0]
    sm1 = jnp.pad(st[:, 1:2], ((0, 0), (0, sseq - 1), (0, 0))).reshape(ts, D_CONV)
    sm2 = jnp.pad(st, ((0, 0), (0, sseq - 2), (0, 0))).reshape(ts, D_CONV)
    ycn, qkv, og, gates, u_s = _premix_sample(xs, sh1s, sc1s, sm1, sm2, w, sseq)

    def heads(a):
        return a.astype(F32).reshape(ns, sseq, HEADS, DH).transpose(0, 2, 1, 3).reshape(ns * HEADS, sseq, DH)

    q, k, v = (heads(qkv[:, i * D_MLSTM:(i + 1) * D_MLSTM]) for i in range(3))
    gate_t = gates.reshape(ns, sseq, 2, HEADS).transpose(2, 0, 3, 1).reshape(2, ns * HEADS, 1, sseq)
    gml_g = jnp.tile(g_mlstm[0].reshape(HEADS, 1, DH), (ns, 1, 1))
    hm, c_s, n_s, m_s = _mlstm_sample(
        q, q.transpose(0, 2, 1), k, k.transpose(0, 2, 1), v, heads(og), gate_t[0], gate_t[1], gml_g,
        state_mlstm_C[0].reshape(ns * HEADS, DH, DH), state_mlstm_n[0].reshape(ns * HEADS, 1, DH),
        state_mlstm_m[0].reshape(ns * HEADS, 1, 1), sseq)
    hmn = hm.reshape(ns, HEADS, sseq, DH).transpose(0, 2, 1, 3).reshape(ts, D_MLSTM).astype(BF16)
    per_tok = lambda a: a.reshape(ts // POST_ROWS, POST_ROWS, D_MODEL)
    x1s, h2t = _postmix(xs, ycn, hmn, per_tok(gt1s), per_tok(sc2s), per_tok(sh2s), w["wo"], w["gffn"], POST_ROWS,
                        tp + ts, tp, h2t)

    p = _peer(h2t, w["wpq"], w["keys"], w["u"], w["vt"])
    y_prompt = _final(x1p, p, gt2p, w["gfin"], seq, 0).reshape(nb, seq, D_MODEL)
    y_sample = _final(x1s, p, per_tok(gt2s), w["gfin"], POST_ROWS, tp).reshape(ns, sseq, D_MODEL)
    conv_s = u_s.reshape(ns, sseq, D_CONV)[:, sseq - 2:][None]
    c_s = c_s.reshape(ns, HEADS, DH, DH)[None]
    n_s = n_s.reshape(ns, HEADS, DH)[None]
    m_s = m_s.reshape(ns, HEADS)[None]

    return (y_prompt, y_sample, conv_p, c_p, n_p, m_p, conv_s, c_s, n_s, m_s)
```

```python
import functools

import jax
import jax.numpy as jnp
import numpy as np
from jax import lax
from jax.experimental import pallas as pl
from jax.experimental.pallas import tpu as pltpu

F32 = jnp.float32
BF16 = jnp.bfloat16
NEG_INF = float("-inf")
EPS = 1e-6

D_MODEL = 1024
D_CONV = 512
CONV_GROUPS = 8
D_MLSTM = 512
HEADS = 4
DH = 128
D_MAIN = 3 * D_CONV + 4 * D_MLSTM
PEER_HEADS = 8
NKEYS = 128
TOPK = 16
NEXP = NKEYS * NKEYS

VMEM_LIMIT = 56 * 1024 * 1024

PREMIX_ROWS = 256
CHUNK = 256
POST_ROWS = 512
EXPERT_COLS = 512
EXPERT_ROWS = 1024
SAMPLE_GROUP = 32


def _dot(a, b):
    return jnp.dot(a, b, preferred_element_type=F32)


def _dot_nt(a, b):
    return lax.dot_general(a, b, (((1,), (1,)), ((), ())), preferred_element_type=F32)


def _dot_tn(a, b):
    return lax.dot_general(a, b, (((0,), (0,)), ((), ())), preferred_element_type=F32)


def _dot_f32(a, b):
    return jnp.dot(a, b, precision=lax.Precision.HIGHEST, preferred_element_type=F32)


def _params(*sem):
    return pltpu.CompilerParams(dimension_semantics=sem, vmem_limit_bytes=VMEM_LIMIT)


def _rms(x):
    return x * lax.rsqrt(jnp.mean(x * x, axis=-1, keepdims=True) + EPS)


def _mod_kernel(c_ref, w_ref, b_ref, o_ref):
    c = c_ref[...]
    a = (c * jax.nn.sigmoid(c)).astype(BF16)
    o_ref[...] = _dot(a, w_ref[...].astype(BF16)) + b_ref[...]


def _mod_rows(c, w_mod, b_mod):
    n = c.shape[0]
    tn = 1024
    return pl.pallas_call(
        _mod_kernel,
        grid=(w_mod.shape[1] // tn,),
        in_specs=[pl.BlockSpec((n, D_MODEL), lambda j: (0, 0)),
                  pl.BlockSpec((D_MODEL, tn), lambda j: (0, j)),
                  pl.BlockSpec((1, tn), lambda j: (0, j))],
        out_specs=pl.BlockSpec((n, tn), lambda j: (0, j)),
        out_shape=jax.ShapeDtypeStruct((n, w_mod.shape[1]), F32),
        compiler_params=_params("arbitrary"),
        name="mod_rows",
    )(c, w_mod, b_mod.reshape(1, -1))


def _log_sigmoid(x):
    return jnp.minimum(x, 0.0) - jnp.log1p(jnp.exp(-jnp.abs(x)))


def _premix_core(x, sh, sc, gmix, win_ref, wg_ref, bif):
    h = _rms(x) * gmix * (1.0 + sc) + sh
    z = _dot(h.astype(BF16), win_ref[...])
    pre = _dot_f32(h, wg_ref[...]) + bif
    lane = lax.broadcasted_iota(jnp.int32, pre.shape, 1)
    gates = jnp.where(lane < HEADS, pre, _log_sigmoid(pre))
    return z, gates


def _conv_tail(z, um1, um2, u, wconv, gconv, gmat_ref, ycn_ref, qkv_ref, og_ref):
    bg = z[:, D_CONV:2 * D_CONV]
    conv = wconv[0:1] * um2 + wconv[1:2] * um1 + wconv[2:3] * u
    yc = bg * conv
    ysq = yc * yc
    hi = ysq.astype(BF16)
    lo = (ysq - hi.astype(F32)).astype(BF16)
    gsum = _dot(hi, gmat_ref[...]) + _dot(lo, gmat_ref[...])
    ycn_ref[...] = (yc * lax.rsqrt(gsum * (CONV_GROUPS / D_CONV) + EPS) * gconv).astype(BF16)
    o0 = 3 * D_CONV
    qkv_ref[:, 0:D_MLSTM] = z[:, o0:o0 + D_MLSTM].astype(BF16)
    qkv_ref[:, D_MLSTM:2 * D_MLSTM] = (z[:, o0 + D_MLSTM:o0 + 2 * D_MLSTM] * (DH ** -0.5)).astype(BF16)
    qkv_ref[:, 2 * D_MLSTM:3 * D_MLSTM] = z[:, o0 + 2 * D_MLSTM:o0 + 3 * D_MLSTM].astype(BF16)
    og_ref[...] = jax.nn.sigmoid(z[:, o0 + 3 * D_MLSTM:o0 + 4 * D_MLSTM]).astype(BF16)


def _premix_prompt_kernel(x_ref, sh_ref, sc_ref, gmix_ref, win_ref, wg_ref, wconv_ref, bif_ref, gconv_ref,
                          gmat_ref, ycn_ref, qkv_ref, og_ref, gate_ref, tail_ref, carry_ref):
    rows = x_ref.shape[0]

    @pl.when(pl.program_id(1) == 0)
    def _():
        carry_ref[...] = jnp.zeros_like(carry_ref)

    z, gates = _premix_core(x_ref[...], sh_ref[0], sc_ref[0], gmix_ref[...], win_ref, wg_ref, bif_ref[...])
    gate_ref[...] = gates
    u = z[:, 2 * D_CONV:3 * D_CONV] * z[:, 0:D_CONV]
    prev = carry_ref[...]
    p0, p1 = prev[6:7], prev[7:8]
    ri = lax.broadcasted_iota(jnp.int32, (rows, 1), 0)
    um1 = jnp.where(ri == 0, p1, pltpu.roll(u, 1, 0))
    um2 = jnp.where(ri == 0, p0, jnp.where(ri == 1, p1, pltpu.roll(u, 2, 0)))
    carry_ref[...] = u[rows - 8:rows]
    tail_ref[0] = u[rows - 8:rows]
    _conv_tail(z, um1, um2, u, wconv_ref[...], gconv_ref[...], gmat_ref, ycn_ref, qkv_ref, og_ref)


def _premix_sample_kernel(x_ref, sh_ref, sc_ref, gmix_ref, win_ref, wg_ref, wconv_ref, bif_ref, gconv_ref,
                          gmat_ref, sm1_ref, sm2_ref, ycn_ref, qkv_ref, og_ref, gate_ref, u_ref, *, seq):
    rows = x_ref.shape[0]
    z, gates = _premix_core(x_ref[...], sh_ref[...], sc_ref[...], gmix_ref[...], win_ref, wg_ref, bif_ref[...])
    gate_ref[...] = gates
    u = z[:, 2 * D_CONV:3 * D_CONV] * z[:, 0:D_CONV]
    u_ref[...] = u
    tmod = lax.broadcasted_iota(jnp.int32, (rows, 1), 0) % seq
    um1 = jnp.where(tmod == 0, sm1_ref[...], pltpu.roll(u, 1, 0))
    um2 = jnp.where(tmod < 2, sm2_ref[...], pltpu.roll(u, 2, 0))
    _conv_tail(z, um1, um2, u, wconv_ref[...], gconv_ref[...], gmat_ref, ycn_ref, qkv_ref, og_ref)


def _const_spec(shape):
    nd = len(shape)
    return pl.BlockSpec(shape, lambda *_: (0,) * nd)


def _premix_weights(w):
    return [w["gmix"], w["win"], w["wg"], w["wconv"], w["bif"], w["gconv"], w["gmat"]]


def _premix_weight_specs():
    return [_const_spec((1, D_MODEL)), _const_spec((D_MODEL, D_MAIN)), _const_spec((D_MODEL, 8)),
            _const_spec((3, D_CONV)), _const_spec((1, 8)), _const_spec((1, D_CONV)),
            _const_spec((D_CONV, D_CONV))]


def _premix_prompt(x, sh, sc, w, nb, seq):
    t = x.shape[0]
    rows = PREMIX_ROWS
    nl = seq // rows
    tok = lambda b, l: (b * nl + l, 0)
    per_seq = pl.BlockSpec((1, 1, D_MODEL), lambda b, l: (b, 0, 0))
    return pl.pallas_call(
        _premix_prompt_kernel,
        grid=(nb, nl),
        in_specs=[pl.BlockSpec((rows, D_MODEL), tok), per_seq, per_seq] + _premix_weight_specs(),
        out_specs=[pl.BlockSpec((rows, D_CONV), tok), pl.BlockSpec((rows, 3 * D_MLSTM), tok),
                   pl.BlockSpec((rows, D_MLSTM), tok), pl.BlockSpec((rows, 8), tok),
                   pl.BlockSpec((1, 8, D_CONV), lambda b, l: (b, 0, 0))],
        out_shape=[jax.ShapeDtypeStruct((t, D_CONV), BF16), jax.ShapeDtypeStruct((t, 3 * D_MLSTM), BF16),
                   jax.ShapeDtypeStruct((t, D_MLSTM), BF16), jax.ShapeDtypeStruct((t, 8), F32),
                   jax.ShapeDtypeStruct((nb, 8, D_CONV), F32)],
        scratch_shapes=[pltpu.VMEM((8, D_CONV), F32)],
        compiler_params=_params("arbitrary", "arbitrary"),
        name="premix_prompt",
    )(x, sh, sc, *_premix_weights(w))


def _premix_sample(x, sh, sc, sm1, sm2, w, seq):
    t = x.shape[0]
    full = lambda n: _const_spec((t, n))
    return pl.pallas_call(
        functools.partial(_premix_sample_kernel, seq=seq),
        grid=(1,),
        in_specs=[full(D_MODEL), full(D_MODEL), full(D_MODEL)] + _premix_weight_specs()
        + [full(D_CONV), full(D_CONV)],
        out_specs=[full(D_CONV), full(3 * D_MLSTM), full(D_MLSTM), full(8), full(D_CONV)],
        out_shape=[jax.ShapeDtypeStruct((t, D_CONV), BF16), jax.ShapeDtypeStruct((t, 3 * D_MLSTM), BF16),
                   jax.ShapeDtypeStruct((t, D_MLSTM), BF16), jax.ShapeDtypeStruct((t, 8), F32),
                   jax.ShapeDtypeStruct((t, D_CONV), F32)],
        compiler_params=_params("arbitrary"),
        name="premix_sample",
    )(x, sh, sc, *_premix_weights(w), sm1, sm2)


def _mlstm_chunk_kernel(qkv_ref, og_ref, gate_ref, gml_ref, h_ref, c_ref, n_ref, m_ref, c_scr, n_scr, m_scr):
    lc = qkv_ref.shape[0]

    @pl.when(pl.program_id(1) == 0)
    def _():
        c_scr[...] = jnp.zeros_like(c_scr)
        n_scr[...] = jnp.zeros_like(n_scr)
        m_scr[...] = jnp.zeros_like(m_scr)

    gates = gate_ref[...]
    eye = (lax.broadcasted_iota(jnp.int32, (8, 8), 0) == lax.broadcasted_iota(jnp.int32, (8, 8), 1)).astype(F32)
    grow = lax.dot_general(eye, gates, (((1,), (1,)), ((), ())), precision=lax.Precision.HIGHEST,
                           preferred_element_type=F32)
    r = lax.broadcasted_iota(jnp.int32, (lc, lc), 0)
    c = lax.broadcasted_iota(jnp.int32, (lc, lc), 1)
    causal = c <= r
    tril = causal.astype(F32)
    bcol = _dot_f32(tril, gates)
    brow = lax.dot_general(grow, tril, (((1,), (1,)), ((), ())), precision=lax.Precision.HIGHEST,
                           preferred_element_type=F32)

    for h in range(HEADS):
        q = qkv_ref[:, h * DH:(h + 1) * DH]
        k = qkv_ref[:, D_MLSTM + h * DH:D_MLSTM + (h + 1) * DH]
        v = qkv_ref[:, 2 * D_MLSTM + h * DH:2 * D_MLSTM + (h + 1) * DH]
        li_c = gates[:, h:h + 1]
        li_r = grow[h:h + 1, :]
        b_c = bcol[:, HEADS + h:HEADS + h + 1]
        b_r = brow[HEADS + h:HEADS + h + 1, :]
        m_old = m_scr[h][0:1, 0:1]
        c_old = c_scr[h]
        n_old = n_scr[h][0:1, :]

        a = b_c + m_old
        dmat = jnp.where(causal, b_c - b_r + li_r, NEG_INF)
        mt = jnp.maximum(a, jnp.max(dmat, axis=1, keepdims=True))
        s = _dot_nt(q, k) * jnp.exp(dmat - mt)
        w_inter = jnp.exp(a - mt)
        num = w_inter * _dot(q, c_old.astype(BF16)) + _dot(s.astype(BF16), v)
        den = w_inter * jnp.sum(q.astype(F32) * n_old, axis=1, keepdims=True) + jnp.sum(s, axis=1, keepdims=True)
        hh = num / jnp.maximum(jnp.abs(den), jnp.exp(-mt))
        hh = og_ref[:, h * DH:(h + 1) * DH].astype(F32) * hh
        h_ref[:, h * DH:(h + 1) * DH] = (_rms(hh) * gml_ref[:, h * DH:(h + 1) * DH]).astype(BF16)

        bl = b_c[lc - 1:lc, :]
        gl = bl - b_c + li_c
        m_new = jnp.maximum(bl + m_old, jnp.max(gl, axis=0, keepdims=True))
        w_old = jnp.exp(bl + m_old - m_new)
        kw = k.astype(F32) * jnp.exp(gl - m_new)
        c_new = w_old * c_old + _dot_tn(kw.astype(BF16), v)
        n_new = w_old * n_old + jnp.sum(kw, axis=0, keepdims=True)
        c_scr[h] = c_new
        n_scr[h] = jnp.broadcast_to(n_new, (8, DH))
        m_scr[h] = jnp.broadcast_to(m_new, (8, DH))
        c_ref[0, h] = c_new
        n_ref[0, h] = jnp.broadcast_to(n_new, (8, DH))
        m_ref[0, h] = jnp.broadcast_to(m_new, (8, DH))


def _mlstm_prompt(qkv, og, gates, gml, nb, seq):
    t = qkv.shape[0]
    nc = seq // CHUNK
    tok = lambda b, c: (b * nc + c, 0)
    state = lambda b, c: (b, 0, 0, 0)
    return pl.pallas_call(
        _mlstm_chunk_kernel,
        grid=(nb, nc),
        in_specs=[pl.BlockSpec((CHUNK, 3 * D_MLSTM), tok), pl.BlockSpec((CHUNK, D_MLSTM), tok),
                  pl.BlockSpec((CHUNK, 8), tok), _const_spec((1, D_MLSTM))],
        out_specs=[pl.BlockSpec((CHUNK, D_MLSTM), tok), pl.BlockSpec((1, HEADS, DH, DH), state),
                   pl.BlockSpec((1, HEADS, 8, DH), state), pl.BlockSpec((1, HEADS, 8, DH), state)],
        out_shape=[jax.ShapeDtypeStruct((t, D_MLSTM), BF16), jax.ShapeDtypeStruct((nb, HEADS, DH, DH), F32),
                   jax.ShapeDtypeStruct((nb, HEADS, 8, DH), F32), jax.ShapeDtypeStruct((nb, HEADS, 8, DH), F32)],
        scratch_shapes=[pltpu.VMEM((HEADS, DH, DH), F32), pltpu.VMEM((HEADS, 8, DH), F32),
                        pltpu.VMEM((HEADS, 8, DH), F32)],
        compiler_params=_params("arbitrary", "arbitrary"),
        name="mlstm_prompt",
    )(qkv, og, gates, gml)


def _mlstm_step_kernel(q_ref, qt_ref, k_ref, kt_ref, v_ref, og_ref, li_ref, lf_ref, gml_ref, c0_ref, n0_ref,
                       m0_ref, h_ref, c_ref, n_ref, m_ref, *, seq):
    c = c0_ref[...]
    n = n0_ref[...]
    m = m0_ref[...]
    for t in range(seq):
        lit = li_ref[:, :, t:t + 1]
        lft = lf_ref[:, :, t:t + 1]
        m_new = jnp.maximum(lft + m, lit)
        fw = jnp.exp(lft + m - m_new)
        iw = jnp.exp(lit - m_new)
        c = fw * c + (iw * kt_ref[:, :, t:t + 1]) * v_ref[:, t:t + 1, :]
        n = fw * n + iw * k_ref[:, t:t + 1, :]
        num = jnp.sum(qt_ref[:, :, t:t + 1] * c, axis=1, keepdims=True)
        den = jnp.sum(q_ref[:, t:t + 1, :] * n, axis=2, keepdims=True)
        hh = og_ref[:, t:t + 1, :] * (num / jnp.maximum(jnp.abs(den), jnp.exp(-m_new)))
        h_ref[:, t:t + 1, :] = _rms(hh) * gml_ref[...]
        m = m_new
    c_ref[...] = c
    n_ref[...] = n
    m_ref[...] = m


def _mlstm_sample(q, qt, k, kt, v, og, li, lf, gml, c0, n0, m0, seq):
    ng = q.shape[0]
    g = SAMPLE_GROUP
    blk = lambda *shape: pl.BlockSpec((g,) + shape, lambda i: (i,) + (0,) * len(shape))
    return pl.pallas_call(
        functools.partial(_mlstm_step_kernel, seq=seq),
        grid=(ng // g,),
        in_specs=[blk(seq, DH), blk(DH, seq), blk(seq, DH), blk(DH, seq), blk(seq, DH), blk(seq, DH),
                  blk(1, seq), blk(1, seq), blk(1, DH), blk(DH, DH), blk(1, DH), blk(1, 1)],
        out_specs=[blk(seq, DH), blk(DH, DH), blk(1, DH), blk(1, 1)],
        out_shape=[jax.ShapeDtypeStruct((ng, seq, DH), F32), jax.ShapeDtypeStruct((ng, DH, DH), F32),
                   jax.ShapeDtypeStruct((ng, 1, DH), F32), jax.ShapeDtypeStruct((ng, 1, 1), F32)],
        compiler_params=_params("arbitrary"),
        name="mlstm_sample",
    )(q, qt, k, kt, v, og, li, lf, gml, c0, n0, m0)


def _postmix_kernel(x_ref, ycn_ref, hmn_ref, gt_ref, sc_ref, sh_ref, wo_ref, gffn_ref, *rest):
    x1_ref, h2t_ref = rest[-2:]
    y = _dot(ycn_ref[...], wo_ref[0:D_CONV, :]) + _dot(hmn_ref[...], wo_ref[D_CONV:D_CONV + D_MLSTM, :])
    x1 = x_ref[...] + gt_ref[0] * y
    x1_ref[...] = x1
    h2 = _rms(x1) * gffn_ref[...] * (1.0 + sc_ref[0]) + sh_ref[0]
    h2t_ref[...] = h2.T.astype(BF16)


def _postmix(x, ycn, hmn, gt, sc, sh, wo, gffn, rows_per_mod, total, col0, h2t_buf=None):
    t = x.shape[0]
    rows = POST_ROWS
    r = gt.shape[1]
    per = rows_per_mod // rows
    tok = lambda i: (i, 0)
    mod = pl.BlockSpec((1, r, D_MODEL), lambda i: (i // per, 0, 0))
    in_specs = [pl.BlockSpec((rows, D_MODEL), tok), pl.BlockSpec((rows, D_CONV), tok),
                pl.BlockSpec((rows, D_MLSTM), tok), mod, mod, mod,
                _const_spec((D_MODEL, D_MODEL)), _const_spec((1, D_MODEL))]
    args = [x, ycn, hmn, gt, sc, sh, wo, gffn]
    aliases = {}
    if h2t_buf is not None:
        in_specs.append(pl.BlockSpec(memory_space=pl.ANY))
        args.append(h2t_buf)
        aliases = {len(args) - 1: 1}
    return pl.pallas_call(
        _postmix_kernel,
        grid=(t // rows,),
        in_specs=in_specs,
        out_specs=[pl.BlockSpec((rows, D_MODEL), tok),
                   pl.BlockSpec((D_MODEL, rows), lambda i: (0, col0 // rows + i))],
        out_shape=[jax.ShapeDtypeStruct((t, D_MODEL), F32), jax.ShapeDtypeStruct((D_MODEL, total), BF16)],
        input_output_aliases=aliases,
        compiler_params=_params("arbitrary"),
        name="postmix",
    )(*args)


def _odd_even_merge_sort_pairs(n):
    pairs = []

    def merge(lo, m, r):
        step = r * 2
        if step < m:
            merge(lo, m, step)
            merge(lo + r, m, step)
            pairs.extend((i, i + r) for i in range(lo + r, lo + m - r, step))
        else:
            pairs.append((lo, lo + r))

    def sort(lo, m):
        if m > 1:
            sort(lo, m // 2)
            sort(lo + m // 2, m // 2)
            merge(lo, m, 1)

    sort(0, n)
    return pairs


_SORT16 = _odd_even_merge_sort_pairs(TOPK)
_BITONIC16 = [(i, i + d) for d in (8, 4, 2, 1) for i in range(TOPK) if not i & d]
SUBLANES = 8


def _exchange(x, pairs):
    for i, j in pairs:
        if x[j] is None:
            continue
        if x[i] is None:
            x[i], x[j] = x[j], None
        else:
            x[i], x[j] = jnp.maximum(x[i], x[j]), jnp.minimum(x[i], x[j])


def _top16(blocks):
    x = list(blocks) + [None] * (TOPK - len(blocks))
    _exchange(x, _SORT16)
    for d in (4, 2, 1):
        p = [None if b is None else pltpu.roll(b, d, 0) for b in x]
        merged = []
        for r in range(TOPK):
            a, b = x[r], p[TOPK - 1 - r]
            merged.append(b if a is None else a if b is None else jnp.maximum(a, b))
        x = merged
        _exchange(x, _BITONIC16)
    return x


def _row_total(x):
    for d in (4, 2, 1):
        x = x + pltpu.roll(x, d, 0)
    return x


def _row_min(x):
    for d in (4, 2, 1):
        x = jnp.minimum(x, pltpu.roll(x, d, 0))
    return x


def _route_head(s1, s2):
    cols = s1.shape[1]
    nb = NKEYS // SUBLANES
    b1 = [s1[SUBLANES * r:SUBLANES * (r + 1)] for r in range(nb)]
    b2 = [s2[SUBLANES * r:SUBLANES * (r + 1)] for r in range(nb)]
    t1 = _top16(b1)
    t2 = _top16(b2)
    sub = lax.broadcasted_iota(jnp.int32, (SUBLANES, cols), 0)
    t2_lo, t2_hi, t1_hi = t2[0], t2[8], t1[8]
    for b in range(1, SUBLANES):
        t2_lo = jnp.where(sub == b, t2[b], t2_lo)
        t2_hi = jnp.where(sub == b, t2[8 + b], t2_hi)
        t1_hi = jnp.where(sub == b, t1[8 + b], t1_hi)
    stair = [t1[0] + t2_lo, t1[0] + t2_hi]
    stair += [jnp.where(sub < TOPK // (a + 1), t1[a] + t2_lo, NEG_INF) for a in range(1, 8)]
    stair += [t1_hi + t2[0]]
    tau = _top16(stair)[TOPK - 1]
    top = t1[0] + t2[0]
    z = sum(jnp.where(cand >= tau, jnp.exp(cand - top), 0.0) for cand in stair)
    inv_z = 1.0 / _row_total(z)
    e2t_lo, e2t_hi = jnp.exp(t2_lo - t2[0]), jnp.exp(t2_hi - t2[0])
    inf = float("inf")
    thr = [_row_min(jnp.minimum(jnp.where(stair[0] >= tau, e2t_lo, inf), jnp.where(stair[1] >= tau, e2t_hi, inf)))]
    thr += [_row_min(jnp.where(cand >= tau, e2t_lo, inf)) for cand in stair[2:9]]
    e2t_top = jnp.exp(t2[0] - t2[0])
    thr += [jnp.where(t1[a] + t2[0] >= tau, e2t_top, inf) for a in range(8, TOPK)]
    thr1, c1, e2 = [], [], []
    for r in range(nb):
        c = jnp.full((SUBLANES, cols), inf, F32)
        for a in reversed(range(TOPK)):
            c = jnp.where(b1[r] == t1[a], thr[a], c)
        thr1.append(c)
        c1.append(jnp.exp(b1[r] - t1[0]) * (0.5 * inv_z))
        e2.append(jnp.exp(b2[r] - t2[0]))
    return thr1, c1, e2


def _gelu_tanh_x2(a):
    c = 0.7978845608028654
    t = jnp.tanh(a * (c + (c * 0.044715) * (a * a)))
    return a + a * t


PACK = 16
LANES = 128


PEER_STEPS = NEXP // (2 * EXPERT_ROWS)
VALUE_MATMUL_AFTER = (1, 4)


def _peer_kernel(h2t_ref, wpq_ref, keys_ref, u_ref, vta_ref, vtb_ref, o_ref,
                 thr1_ref, c1_ref, e2_ref, acc_ref, ga_ref, gb_ref):
    s = pl.program_id(1)
    ht = h2t_ref[...]
    cols = ht.shape[1]

    @pl.when(s == 0)
    def _():
        acc_ref[...] = jnp.zeros_like(acc_ref)
        gb_ref[...] = jnp.zeros_like(gb_ref)

        def head(h, carry):
            row0 = pl.multiple_of(h * 2 * NKEYS, 2 * NKEYS)
            qt = _dot(wpq_ref[pl.ds(row0, 2 * NKEYS), :], ht).astype(BF16)
            s1 = _dot(keys_ref[h], qt[0:NKEYS])
            s2 = _dot(keys_ref[PEER_HEADS + h], qt[NKEYS:2 * NKEYS])
            thr1, c1, e2 = _route_head(s1, s2)
            for r in range(NKEYS // SUBLANES):
                for cc in range(cols // LANES):
                    lanes = slice(cc * LANES, (cc + 1) * LANES)
                    thr1_ref[r, cc, h] = thr1[r][:, lanes]
                    c1_ref[r, cc, h] = c1[r][:, lanes]
                    e2_ref[h, cc, SUBLANES * r:SUBLANES * (r + 1), :] = e2[r][:, lanes]
            return carry

        lax.fori_loop(0, PEER_HEADS, head, 0)

    n_i1 = EXPERT_ROWS // NKEYS

    def build(tile, half, g_ref, vt_ref, g_done_ref):
        for j in range(n_i1):
            u0 = half * EXPERT_ROWS + j * NKEYS
            a = _dot(u_ref[u0:u0 + NKEYS, :], ht)
            if j in VALUE_MATMUL_AFTER:
                i = VALUE_MATMUL_AFTER.index(j)
                n = D_MODEL // len(VALUE_MATMUL_AFTER)
                acc_ref[i * n:(i + 1) * n, :] += _dot(vt_ref[0, i * n:(i + 1) * n, :], g_done_ref[...])
            for cc in range(cols // LANES):
                lanes = slice(cc * LANES, (cc + 1) * LANES)
                w = [None] * (NKEYS // SUBLANES)
                for h in range(PEER_HEADS):
                    thr = thr1_ref[tile, cc, h, j:j + 1, :]
                    c1 = c1_ref[tile, cc, h, j:j + 1, :]
                    for k in range(NKEYS // SUBLANES):
                        e2 = e2_ref[h, cc, k * SUBLANES:(k + 1) * SUBLANES, :]
                        term = jnp.where(e2 >= thr, e2, 0.0) * c1
                        w[k] = term if h == 0 else w[k] + term
                for k in range(NKEYS // PACK):
                    rows = slice(k * PACK, (k + 1) * PACK)
                    gate = jnp.concatenate(w[2 * k:2 * k + 2], axis=0)
                    g_ref[j * NKEYS + k * PACK:j * NKEYS + (k + 1) * PACK, lanes] = (
                        gate * _gelu_tanh_x2(a[rows, lanes])).astype(BF16)

    @pl.when(s < PEER_STEPS)
    def _():
        build(2 * s, 0, ga_ref, vta_ref, gb_ref)
        build(2 * s + 1, 1, gb_ref, vtb_ref, ga_ref)

    @pl.when(s == PEER_STEPS)
    def _():
        o_ref[...] = (acc_ref[...] + _dot(vta_ref[0], gb_ref[...])).T


def _peer(h2t, wpq, keys, u, vt):
    t = h2t.shape[1]
    cols = EXPERT_COLS
    last = NEXP // EXPERT_ROWS - 1
    return pl.pallas_call(
        _peer_kernel,
        grid=(t // cols, PEER_STEPS + 1),
        in_specs=[pl.BlockSpec((D_MODEL, cols), lambda i, s: (0, i)),
                  _const_spec((PEER_HEADS * 2 * NKEYS, D_MODEL)),
                  _const_spec((2 * PEER_HEADS, NKEYS, NKEYS)),
                  pl.BlockSpec((2 * EXPERT_ROWS, D_MODEL), lambda i, s: (jnp.minimum(s, PEER_STEPS - 1), 0)),
                  pl.BlockSpec((1, D_MODEL, EXPERT_ROWS), lambda i, s: (jnp.maximum(2 * s - 1, 0), 0, 0)),
                  pl.BlockSpec((1, D_MODEL, EXPERT_ROWS), lambda i, s: (jnp.minimum(2 * s, last), 0, 0))],
        out_specs=pl.BlockSpec((cols, D_MODEL), lambda i, s: (i, 0)),
        out_shape=jax.ShapeDtypeStruct((t, D_MODEL), F32),
        scratch_shapes=[pltpu.VMEM((NKEYS // SUBLANES, cols // LANES, PEER_HEADS, SUBLANES, LANES), F32),
                        pltpu.VMEM((NKEYS // SUBLANES, cols // LANES, PEER_HEADS, SUBLANES, LANES), F32),
                        pltpu.VMEM((PEER_HEADS, cols // LANES, NKEYS, LANES), F32),
                        pltpu.VMEM((D_MODEL, cols), F32),
                        pltpu.VMEM((EXPERT_ROWS, cols), BF16), pltpu.VMEM((EXPERT_ROWS, cols), BF16)],
        compiler_params=_params("arbitrary", "arbitrary"),
        name="peer",
    )(h2t, wpq, keys, u, vt, vt)


def _final_kernel(x1_ref, p_ref, gt_ref, gfin_ref, y_ref):
    y_ref[...] = _rms(x1_ref[...] + gt_ref[0] * p_ref[...]) * gfin_ref[...]


def _final(x1, p, gt, gfin, rows_per_mod, row0):
    t = x1.shape[0]
    rows = POST_ROWS
    r = gt.shape[1]
    per = rows_per_mod // rows
    tok = pl.BlockSpec((rows, D_MODEL), lambda i: (i, 0))
    return pl.pallas_call(
        _final_kernel,
        grid=(t // rows,),
        in_specs=[tok, pl.BlockSpec((rows, D_MODEL), lambda i: (row0 // rows + i, 0)),
                  pl.BlockSpec((1, r, D_MODEL), lambda i: (i // per, 0, 0)), _const_spec((1, D_MODEL))],
        out_specs=tok,
        out_shape=jax.ShapeDtypeStruct((t, D_MODEL), F32),
        compiler_params=_params("arbitrary"),
        name="final_norm",
    )(x1, p, gt, gfin)


def _group_matrix():
    g = np.arange(D_CONV) // (D_CONV // CONV_GROUPS)
    return jnp.asarray(g[:, None] == g[None, :], dtype=BF16)


def kernel(x_prompt, x_sample, c_prompt, c_sample, state_conv, state_mlstm_C, state_mlstm_n, state_mlstm_m,
           w_mod, b_mod, g_mix, w_in, w_conv, b_i, b_f, g_conv, g_mlstm, w_out, g_ffn, w_pq, sub_keys,
           u_tab, v_tab, g_final):
    depth = w_mod.shape[0]
    assert depth == 1
    nb, seq, _ = x_prompt.shape
    ns, sseq, _ = x_sample.shape
    ts = ns * sseq

    w = {
        "gmix": g_mix[0].reshape(1, -1),
        "win": w_in[0][:, :D_MAIN].astype(BF16),
        "wg": w_in[0][:, D_MAIN:],
        "wconv": w_conv[0],
        "bif": jnp.concatenate([b_i[0], b_f[0]]).reshape(1, 8),
        "gconv": g_conv[0].reshape(1, -1),
        "gmat": _group_matrix(),
        "gml": g_mlstm[0].reshape(1, -1),
        "wo": w_out[0].astype(BF16),
        "gffn": g_ffn[0].reshape(1, -1),
        "wpq": w_pq[0].T.astype(BF16),
        "keys": sub_keys[0].reshape(2 * PEER_HEADS, NKEYS, -1).astype(BF16),
        "u": u_tab[0].astype(BF16),
        "vt": v_tab[0].reshape(NEXP // EXPERT_ROWS, EXPERT_ROWS, D_MODEL).transpose(0, 2, 1).astype(BF16),
        "gfin": g_final.reshape(1, -1),
    }

    mod = _mod_rows(jnp.concatenate([c_prompt, c_sample], axis=0), w_mod[0], b_mod[0])
    mod_p = mod[:nb].reshape(nb, 1, 6, D_MODEL)
    sh1p, sc1p, gt1p, sh2p, sc2p, gt2p = (mod_p[:, :, i] for i in range(6))
    mod_s = jnp.broadcast_to(mod[nb:].reshape(ns, 1, 6, D_MODEL), (ns, sseq, 6, D_MODEL)).reshape(ts, 6, D_MODEL)
    sh1s, sc1s, gt1s, sh2s, sc2s, gt2s = (mod_s[:, i] for i in range(6))

    xp = x_prompt.reshape(nb * seq, D_MODEL)
    ycn, qkv, og, gates, tail = _premix_prompt(xp, sh1p, sc1p, w, nb, seq)
    hmn, c_p, n_p, m_p = _mlstm_prompt(qkv, og, gates, w["gml"], nb, seq)
    tp = nb * seq
    x1p, h2t = _postmix(xp, ycn, hmn, gt1p, sc2p, sh2p, w["wo"], w["gffn"], seq, tp + ts, 0)
    conv_p = tail[:, 6:8][None]
    c_p = c_p[None]
    n_p = n_p[:, :, 0][None]
    m_p = m_p[:, :, 0, 0][None]

    xs = x_sample.reshape(ts, D_MODEL)
    st = state_conv[0]
    sm1 = jnp.pad(st[:, 1:2], ((0, 0), (0, sseq - 1), (0, 0))).reshape(ts, D_CONV)
    sm2 = jnp.pad(st, ((0, 0), (0, sseq - 2), (0, 0))).reshape(ts, D_CONV)
    ycn, qkv, og, gates, u_s = _premix_sample(xs, sh1s, sc1s, sm1, sm2, w, sseq)

    def heads(a):
        return a.astype(F32).reshape(ns, sseq, HEADS, DH).transpose(0, 2, 1, 3).reshape(ns * HEADS, sseq, DH)

    q, k, v = (heads(qkv[:, i * D_MLSTM:(i + 1) * D_MLSTM]) for i in range(3))
    gate_t = gates.reshape(ns, sseq, 2, HEADS).transpose(2, 0, 3, 1).reshape(2, ns * HEADS, 1, sseq)
    gml_g = jnp.tile(g_mlstm[0].reshape(HEADS, 1, DH), (ns, 1, 1))
    hm, c_s, n_s, m_s = _mlstm_sample(
        q, q.transpose(0, 2, 1), k, k.transpose(0, 2, 1), v, heads(og), gate_t[0], gate_t[1], gml_g,
        state_mlstm_C[0].reshape(ns * HEADS, DH, DH), state_mlstm_n[0].reshape(ns * HEADS, 1, DH),
        state_mlstm_m[0].reshape(ns * HEADS, 1, 1), sseq)
    hmn = hm.reshape(ns, HEADS, sseq, DH).transpose(0, 2, 1, 3).reshape(ts, D_MLSTM).astype(BF16)
    per_tok = lambda a: a.reshape(ts // POST_ROWS, POST_ROWS, D_MODEL)
    x1s, h2t = _postmix(xs, ycn, hmn, per_tok(gt1s), per_tok(sc2s), per_tok(sh2s), w["wo"], w["gffn"], POST_ROWS,
                        tp + ts, tp, h2t)

    p = _peer(h2t, w["wpq"], w["keys"], w["u"], w["vt"])
    y_prompt = _final(x1p, p, gt2p, w["gfin"], seq, 0).reshape(nb, seq, D_MODEL)
    y_sample = _final(x1s, p, per_tok(gt2s), w["gfin"], POST_ROWS, tp).reshape(ns, sseq, D_MODEL)
    conv_s = u_s.reshape(ns, sseq, D_CONV)[:, sseq - 2:][None]
    c_s = c_s.reshape(ns, HEADS, DH, DH)[None]
    n_s = n_s.reshape(ns, HEADS, DH)[None]
    m_s = m_s.reshape(ns, HEADS)[None]

    return (y_prompt, y_sample, conv_p, c_p, n_p, m_p, conv_s, c_s, n_s, m_s)
```

```python
import functools

import jax
import jax.numpy as jnp
import numpy as np
from jax import lax
from jax.experimental import pallas as pl
from jax.experimental.pallas import tpu as pltpu

F32 = jnp.float32
BF16 = jnp.bfloat16
NEG_INF = float("-inf")
EPS = 1e-6

D_MODEL = 1024
D_CONV = 512
CONV_GROUPS = 8
D_MLSTM = 512
HEADS = 4
DH = 128
D_MAIN = 3 * D_CONV + 4 * D_MLSTM
PEER_HEADS = 8
NKEYS = 128
TOPK = 16
NEXP = NKEYS * NKEYS

VMEM_LIMIT = 56 * 1024 * 1024

PREMIX_ROWS = 256
CHUNK = 256
MLSTM_SEQS = 1
POST_ROWS = 512
EXPERT_COLS = 512
EXPERT_ROWS = 1024
SAMPLE_GROUP = 32


def _dot(a, b):
    return jnp.dot(a, b, preferred_element_type=F32)


def _dot_nt(a, b):
    return lax.dot_general(a, b, (((1,), (1,)), ((), ())), preferred_element_type=F32)


def _dot_tn(a, b):
    return lax.dot_general(a, b, (((0,), (0,)), ((), ())), preferred_element_type=F32)


def _dot_f32(a, b):
    return jnp.dot(a, b, precision=lax.Precision.HIGHEST, preferred_element_type=F32)


def _params(*sem):
    return pltpu.CompilerParams(dimension_semantics=sem, vmem_limit_bytes=VMEM_LIMIT)


def _rms(x):
    return x * lax.rsqrt(jnp.mean(x * x, axis=-1, keepdims=True) + EPS)


def _mod_kernel(c_ref, w_ref, b_ref, o_ref):
    c = c_ref[...]
    a = (c * jax.nn.sigmoid(c)).astype(BF16)
    o_ref[...] = _dot(a, w_ref[...].astype(BF16)) + b_ref[...]


def _mod_rows(c, w_mod, b_mod):
    n = c.shape[0]
    tn = 1024
    return pl.pallas_call(
        _mod_kernel,
        grid=(w_mod.shape[1] // tn,),
        in_specs=[pl.BlockSpec((n, D_MODEL), lambda j: (0, 0)),
                  pl.BlockSpec((D_MODEL, tn), lambda j: (0, j)),
                  pl.BlockSpec((1, tn), lambda j: (0, j))],
        out_specs=pl.BlockSpec((n, tn), lambda j: (0, j)),
        out_shape=jax.ShapeDtypeStruct((n, w_mod.shape[1]), F32),
        compiler_params=_params("arbitrary"),
        name="mod_rows",
    )(c, w_mod, b_mod.reshape(1, -1))


def _log_sigmoid(x):
    return jnp.minimum(x, 0.0) - jnp.log1p(jnp.exp(-jnp.abs(x)))


def _premix_core(x, sh, sc, gmix, win_ref, wg_ref, bif):
    h = _rms(x) * gmix * (1.0 + sc) + sh
    z = _dot(h.astype(BF16), win_ref[...])
    lane = lax.broadcasted_iota(jnp.int32, (x.shape[0], 2 * HEADS), 1)
    pre = bif
    for j in range(2 * HEADS):
        col = jnp.sum(h * wg_ref[j:j + 1, :], axis=1, keepdims=True)
        pre = pre + jnp.where(lane == j, col, 0.0)
    gates = jnp.where(lane < HEADS, pre, _log_sigmoid(pre))
    return z, gates


def _conv_tail(z, um1, um2, u, wconv, gconv, gmat_ref, ycn_ref, qkv_ref, og_ref):
    bg = z[:, D_CONV:2 * D_CONV]
    conv = wconv[0:1] * um2 + wconv[1:2] * um1 + wconv[2:3] * u
    yc = bg * conv
    ysq = yc * yc
    hi = ysq.astype(BF16)
    lo = (ysq - hi.astype(F32)).astype(BF16)
    gsum = _dot(hi, gmat_ref[...]) + _dot(lo, gmat_ref[...])
    ycn_ref[...] = (yc * lax.rsqrt(gsum * (CONV_GROUPS / D_CONV) + EPS) * gconv).astype(BF16)
    o0 = 3 * D_CONV
    qkv_ref[:, 0:D_MLSTM] = z[:, o0:o0 + D_MLSTM].astype(BF16)
    qkv_ref[:, D_MLSTM:2 * D_MLSTM] = (z[:, o0 + D_MLSTM:o0 + 2 * D_MLSTM] * (DH ** -0.5)).astype(BF16)
    qkv_ref[:, 2 * D_MLSTM:3 * D_MLSTM] = z[:, o0 + 2 * D_MLSTM:o0 + 3 * D_MLSTM].astype(BF16)
    og_ref[...] = jax.nn.sigmoid(z[:, o0 + 3 * D_MLSTM:o0 + 4 * D_MLSTM]).astype(BF16)


def _premix_prompt_kernel(x_ref, sh_ref, sc_ref, gmix_ref, win_ref, wg_ref, wconv_ref, bif_ref, gconv_ref,
                          gmat_ref, ycn_ref, qkv_ref, og_ref, gate_ref, tail_ref, carry_ref):
    rows = x_ref.shape[0]

    @pl.when(pl.program_id(1) == 0)
    def _():
        carry_ref[...] = jnp.zeros_like(carry_ref)

    z, gates = _premix_core(x_ref[...], sh_ref[0], sc_ref[0], gmix_ref[...], win_ref, wg_ref, bif_ref[...])
    gate_ref[...] = gates
    u = z[:, 2 * D_CONV:3 * D_CONV] * z[:, 0:D_CONV]
    prev = carry_ref[...]
    p0, p1 = prev[6:7], prev[7:8]
    ri = lax.broadcasted_iota(jnp.int32, (rows, 1), 0)
    um1 = jnp.where(ri == 0, p1, pltpu.roll(u, 1, 0))
    um2 = jnp.where(ri == 0, p0, jnp.where(ri == 1, p1, pltpu.roll(u, 2, 0)))
    carry_ref[...] = u[rows - 8:rows]
    tail_ref[0] = u[rows - 8:rows]
    _conv_tail(z, um1, um2, u, wconv_ref[...], gconv_ref[...], gmat_ref, ycn_ref, qkv_ref, og_ref)


def _premix_sample_kernel(x_ref, sh_ref, sc_ref, gmix_ref, win_ref, wg_ref, wconv_ref, bif_ref, gconv_ref,
                          gmat_ref, sm1_ref, sm2_ref, ycn_ref, qkv_ref, og_ref, gate_ref, u_ref, *, seq):
    rows = x_ref.shape[0]
    z, gates = _premix_core(x_ref[...], sh_ref[...], sc_ref[...], gmix_ref[...], win_ref, wg_ref, bif_ref[...])
    gate_ref[...] = gates
    u = z[:, 2 * D_CONV:3 * D_CONV] * z[:, 0:D_CONV]
    u_ref[...] = u
    tmod = lax.broadcasted_iota(jnp.int32, (rows, 1), 0) % seq
    um1 = jnp.where(tmod == 0, sm1_ref[...], pltpu.roll(u, 1, 0))
    um2 = jnp.where(tmod < 2, sm2_ref[...], pltpu.roll(u, 2, 0))
    _conv_tail(z, um1, um2, u, wconv_ref[...], gconv_ref[...], gmat_ref, ycn_ref, qkv_ref, og_ref)


def _const_spec(shape):
    nd = len(shape)
    return pl.BlockSpec(shape, lambda *_: (0,) * nd)


def _premix_weights(w):
    return [w["gmix"], w["win"], w["wg"], w["wconv"], w["bif"], w["gconv"], w["gmat"]]


def _premix_weight_specs():
    return [_const_spec((1, D_MODEL)), _const_spec((D_MODEL, D_MAIN)), _const_spec((8, D_MODEL)),
            _const_spec((3, D_CONV)), _const_spec((1, 8)), _const_spec((1, D_CONV)),
            _const_spec((D_CONV, D_CONV))]


def _premix_prompt(x, sh, sc, w, nb, seq):
    t = x.shape[0]
    rows = PREMIX_ROWS
    nl = seq // rows
    tok = lambda b, l: (b * nl + l, 0)
    per_seq = pl.BlockSpec((1, 1, D_MODEL), lambda b, l: (b, 0, 0))
    return pl.pallas_call(
        _premix_prompt_kernel,
        grid=(nb, nl),
        in_specs=[pl.BlockSpec((rows, D_MODEL), tok), per_seq, per_seq] + _premix_weight_specs(),
        out_specs=[pl.BlockSpec((rows, D_CONV), tok), pl.BlockSpec((rows, 3 * D_MLSTM), tok),
                   pl.BlockSpec((rows, D_MLSTM), tok), pl.BlockSpec((rows, 8), tok),
                   pl.BlockSpec((1, 8, D_CONV), lambda b, l: (b, 0, 0))],
        out_shape=[jax.ShapeDtypeStruct((t, D_CONV), BF16), jax.ShapeDtypeStruct((t, 3 * D_MLSTM), BF16),
                   jax.ShapeDtypeStruct((t, D_MLSTM), BF16), jax.ShapeDtypeStruct((t, 8), F32),
                   jax.ShapeDtypeStruct((nb, 8, D_CONV), F32)],
        scratch_shapes=[pltpu.VMEM((8, D_CONV), F32)],
        compiler_params=_params("arbitrary", "arbitrary"),
        name="premix_prompt",
    )(x, sh, sc, *_premix_weights(w))


def _premix_sample(x, sh, sc, sm1, sm2, w, seq):
    t = x.shape[0]
    full = lambda n: _const_spec((t, n))
    return pl.pallas_call(
        functools.partial(_premix_sample_kernel, seq=seq),
        grid=(1,),
        in_specs=[full(D_MODEL), full(D_MODEL), full(D_MODEL)] + _premix_weight_specs()
        + [full(D_CONV), full(D_CONV)],
        out_specs=[full(D_CONV), full(3 * D_MLSTM), full(D_MLSTM), full(8), full(D_CONV)],
        out_shape=[jax.ShapeDtypeStruct((t, D_CONV), BF16), jax.ShapeDtypeStruct((t, 3 * D_MLSTM), BF16),
                   jax.ShapeDtypeStruct((t, D_MLSTM), BF16), jax.ShapeDtypeStruct((t, 8), F32),
                   jax.ShapeDtypeStruct((t, D_CONV), F32)],
        compiler_params=_params("arbitrary"),
        name="premix_sample",
    )(x, sh, sc, *_premix_weights(w), sm1, sm2)


def _mlstm_chunk_kernel(qkv_ref, og_ref, gate_ref, gml_ref, h_ref, c_ref, n_ref, m_ref, c_scr, n_scr, m_scr):
    nseq, lc = qkv_ref.shape[0], qkv_ref.shape[1]

    @pl.when(pl.program_id(1) == 0)
    def _():
        c_scr[...] = jnp.zeros_like(c_scr)
        n_scr[...] = jnp.zeros_like(n_scr)
        m_scr[...] = jnp.zeros_like(m_scr)

    eye = (lax.broadcasted_iota(jnp.int32, (8, 8), 0) == lax.broadcasted_iota(jnp.int32, (8, 8), 1)).astype(F32)
    r = lax.broadcasted_iota(jnp.int32, (lc, lc), 0)
    c = lax.broadcasted_iota(jnp.int32, (lc, lc), 1)
    causal = c <= r
    tril = causal.astype(F32)

    for bb in range(nseq):
        gates = gate_ref[bb]
        grow = lax.dot_general(eye, gates, (((1,), (1,)), ((), ())), precision=lax.Precision.HIGHEST,
                               preferred_element_type=F32)
        bcol = _dot_f32(tril, gates)
        brow = lax.dot_general(grow, tril, (((1,), (1,)), ((), ())), precision=lax.Precision.HIGHEST,
                               preferred_element_type=F32)
        for h in range(HEADS):
            st = bb * HEADS + h
            q = qkv_ref[bb, :, h * DH:(h + 1) * DH]
            k = qkv_ref[bb, :, D_MLSTM + h * DH:D_MLSTM + (h + 1) * DH]
            v = qkv_ref[bb, :, 2 * D_MLSTM + h * DH:2 * D_MLSTM + (h + 1) * DH]
            li_c = gates[:, h:h + 1]
            li_r = grow[h:h + 1, :]
            b_c = bcol[:, HEADS + h:HEADS + h + 1]
            b_r = brow[HEADS + h:HEADS + h + 1, :]
            m_old = m_scr[st][0:1, 0:1]
            c_old = c_scr[st]
            n_old = n_scr[st][0:1, :]

            a = b_c + m_old
            dmat = jnp.where(causal, b_c - b_r + li_r, NEG_INF)
            mt = jnp.maximum(a, jnp.max(dmat, axis=1, keepdims=True))
            s = _dot_nt(q, k) * jnp.exp(dmat - mt)
            w_inter = jnp.exp(a - mt)
            num = w_inter * _dot(q, c_old.astype(BF16)) + _dot(s.astype(BF16), v)
            den = (w_inter * jnp.sum(q.astype(F32) * n_old, axis=1, keepdims=True)
                   + jnp.sum(s, axis=1, keepdims=True))
            hh = num / jnp.maximum(jnp.abs(den), jnp.exp(-mt))
            hh = og_ref[bb, :, h * DH:(h + 1) * DH].astype(F32) * hh
            h_ref[bb, :, h * DH:(h + 1) * DH] = (_rms(hh) * gml_ref[:, h * DH:(h + 1) * DH]).astype(BF16)

            bl = b_c[lc - 1:lc, :]
            gl = bl - b_c + li_c
            m_new = jnp.maximum(bl + m_old, jnp.max(gl, axis=0, keepdims=True))
            w_old = jnp.exp(bl + m_old - m_new)
            kw = k.astype(F32) * jnp.exp(gl - m_new)
            c_new = w_old * c_old + _dot_tn(kw.astype(BF16), v)
            n_new = w_old * n_old + jnp.sum(kw, axis=0, keepdims=True)
            c_scr[st] = c_new
            n_scr[st] = jnp.broadcast_to(n_new, (8, DH))
            m_scr[st] = jnp.broadcast_to(m_new, (8, DH))
            c_ref[bb, h] = c_new
            n_ref[bb, h] = jnp.broadcast_to(n_new, (8, DH))
            m_ref[bb, h] = jnp.broadcast_to(m_new, (8, DH))


def _mlstm_prompt(qkv, og, gates, gml, nb, seq):
    nc = seq // CHUNK
    ns = MLSTM_SEQS
    tok = lambda b, c: (b, c, 0)
    state = lambda b, c: (b, 0, 0, 0)
    per_seq = lambda a: a.reshape(nb, seq, a.shape[-1])
    hmn, c_p, n_p, m_p = pl.pallas_call(
        _mlstm_chunk_kernel,
        grid=(nb // ns, nc),
        in_specs=[pl.BlockSpec((ns, CHUNK, 3 * D_MLSTM), tok), pl.BlockSpec((ns, CHUNK, D_MLSTM), tok),
                  pl.BlockSpec((ns, CHUNK, 8), tok), _const_spec((1, D_MLSTM))],
        out_specs=[pl.BlockSpec((ns, CHUNK, D_MLSTM), tok), pl.BlockSpec((ns, HEADS, DH, DH), state),
                   pl.BlockSpec((ns, HEADS, 8, DH), state), pl.BlockSpec((ns, HEADS, 8, DH), state)],
        out_shape=[jax.ShapeDtypeStruct((nb, seq, D_MLSTM), BF16), jax.ShapeDtypeStruct((nb, HEADS, DH, DH), F32),
                   jax.ShapeDtypeStruct((nb, HEADS, 8, DH), F32), jax.ShapeDtypeStruct((nb, HEADS, 8, DH), F32)],
        scratch_shapes=[pltpu.VMEM((ns * HEADS, DH, DH), F32), pltpu.VMEM((ns * HEADS, 8, DH), F32),
                        pltpu.VMEM((ns * HEADS, 8, DH), F32)],
        compiler_params=_params("arbitrary", "arbitrary"),
        name="mlstm_prompt",
    )(per_seq(qkv), per_seq(og), per_seq(gates), gml)
    return hmn.reshape(nb * seq, D_MLSTM), c_p, n_p, m_p


def _mlstm_step_kernel(q_ref, qt_ref, k_ref, kt_ref, v_ref, og_ref, li_ref, lf_ref, gml_ref, c0_ref, n0_ref,
                       m0_ref, h_ref, c_ref, n_ref, m_ref, *, seq):
    c = c0_ref[...]
    n = n0_ref[...]
    m = m0_ref[...]
    for t in range(seq):
        lit = li_ref[:, :, t:t + 1]
        lft = lf_ref[:, :, t:t + 1]
        m_new = jnp.maximum(lft + m, lit)
        fw = jnp.exp(lft + m - m_new)
        iw = jnp.exp(lit - m_new)
        c = fw * c + (iw * kt_ref[:, :, t:t + 1]) * v_ref[:, t:t + 1, :]
        n = fw * n + iw * k_ref[:, t:t + 1, :]
        num = jnp.sum(qt_ref[:, :, t:t + 1] * c, axis=1, keepdims=True)
        den = jnp.sum(q_ref[:, t:t + 1, :] * n, axis=2, keepdims=True)
        hh = og_ref[:, t:t + 1, :] * (num / jnp.maximum(jnp.abs(den), jnp.exp(-m_new)))
        h_ref[:, t:t + 1, :] = _rms(hh) * gml_ref[...]
        m = m_new
    c_ref[...] = c
    n_ref[...] = n
    m_ref[...] = m


def _mlstm_sample(q, qt, k, kt, v, og, li, lf, gml, c0, n0, m0, seq):
    ng = q.shape[0]
    g = SAMPLE_GROUP
    blk = lambda *shape: pl.BlockSpec((g,) + shape, lambda i: (i,) + (0,) * len(shape))
    return pl.pallas_call(
        functools.partial(_mlstm_step_kernel, seq=seq),
        grid=(ng // g,),
        in_specs=[blk(seq, DH), blk(DH, seq), blk(seq, DH), blk(DH, seq), blk(seq, DH), blk(seq, DH),
                  blk(1, seq), blk(1, seq), blk(1, DH), blk(DH, DH), blk(1, DH), blk(1, 1)],
        out_specs=[blk(seq, DH), blk(DH, DH), blk(1, DH), blk(1, 1)],
        out_shape=[jax.ShapeDtypeStruct((ng, seq, DH), F32), jax.ShapeDtypeStruct((ng, DH, DH), F32),
                   jax.ShapeDtypeStruct((ng, 1, DH), F32), jax.ShapeDtypeStruct((ng, 1, 1), F32)],
        compiler_params=_params("arbitrary"),
        name="mlstm_sample",
    )(q, qt, k, kt, v, og, li, lf, gml, c0, n0, m0)


def _postmix_kernel(x_ref, ycn_ref, hmn_ref, gt_ref, sc_ref, sh_ref, wo_ref, gffn_ref, x1_ref, h2t_ref):
    y = _dot(ycn_ref[...], wo_ref[0:D_CONV, :]) + _dot(hmn_ref[...], wo_ref[D_CONV:D_CONV + D_MLSTM, :])
    x1 = x_ref[...] + gt_ref[0] * y
    x1_ref[...] = x1
    h2 = _rms(x1) * gffn_ref[...] * (1.0 + sc_ref[0]) + sh_ref[0]
    h2t_ref[...] = h2.T.astype(BF16)


def _postmix(x, ycn, hmn, gt, sc, sh, wo, gffn, rows_per_mod):
    t = x.shape[0]
    rows = POST_ROWS
    r = gt.shape[1]
    per = rows_per_mod // rows
    tok = lambda i: (i, 0)
    mod = pl.BlockSpec((1, r, D_MODEL), lambda i: (i // per, 0, 0))
    return pl.pallas_call(
        _postmix_kernel,
        grid=(t // rows,),
        in_specs=[pl.BlockSpec((rows, D_MODEL), tok), pl.BlockSpec((rows, D_CONV), tok),
                  pl.BlockSpec((rows, D_MLSTM), tok), mod, mod, mod,
                  _const_spec((D_MODEL, D_MODEL)), _const_spec((1, D_MODEL))],
        out_specs=[pl.BlockSpec((rows, D_MODEL), tok), pl.BlockSpec((D_MODEL, rows), lambda i: (0, i))],
        out_shape=[jax.ShapeDtypeStruct((t, D_MODEL), F32), jax.ShapeDtypeStruct((D_MODEL, t), BF16)],
        compiler_params=_params("arbitrary"),
        name="postmix",
    )(x, ycn, hmn, gt, sc, sh, wo, gffn)


def _odd_even_merge_sort_pairs(n):
    pairs = []

    def merge(lo, m, r):
        step = r * 2
        if step < m:
            merge(lo, m, step)
            merge(lo + r, m, step)
            pairs.extend((i, i + r) for i in range(lo + r, lo + m - r, step))
        else:
            pairs.append((lo, lo + r))

    def sort(lo, m):
        if m > 1:
            sort(lo, m // 2)
            sort(lo + m // 2, m // 2)
            merge(lo, m, 1)

    sort(0, n)
    return pairs


_SORT16 = _odd_even_merge_sort_pairs(TOPK)
_BITONIC16 = [(i, i + d) for d in (8, 4, 2, 1) for i in range(TOPK) if not i & d]
SUBLANES = 8


def _exchange(x, pairs):
    for i, j in pairs:
        if x[j] is None:
            continue
        if x[i] is None:
            x[i], x[j] = x[j], None
        else:
            x[i], x[j] = jnp.maximum(x[i], x[j]), jnp.minimum(x[i], x[j])


def _top16(blocks):
    x = list(blocks) + [None] * (TOPK - len(blocks))
    _exchange(x, _SORT16)
    for d in (4, 2, 1):
        p = [None if b is None else pltpu.roll(b, d, 0) for b in x]
        merged = []
        for r in range(TOPK):
            a, b = x[r], p[TOPK - 1 - r]
            merged.append(b if a is None else a if b is None else jnp.maximum(a, b))
        x = merged
        _exchange(x, _BITONIC16)
    return x


def _row_total(x):
    for d in (4, 2, 1):
        x = x + pltpu.roll(x, d, 0)
    return x


def _row_min(x):
    for d in (4, 2, 1):
        x = jnp.minimum(x, pltpu.roll(x, d, 0))
    return x


def _route_head(s1, s2):
    cols = s1.shape[1]
    nb = NKEYS // SUBLANES
    b1 = [s1[SUBLANES * r:SUBLANES * (r + 1)] for r in range(nb)]
    b2 = [s2[SUBLANES * r:SUBLANES * (r + 1)] for r in range(nb)]
    t1 = _top16(b1)
    t2 = _top16(b2)
    sub = lax.broadcasted_iota(jnp.int32, (SUBLANES, cols), 0)
    t2_lo, t2_hi, t1_hi = t2[0], t2[8], t1[8]
    for b in range(1, SUBLANES):
        t2_lo = jnp.where(sub == b, t2[b], t2_lo)
        t2_hi = jnp.where(sub == b, t2[8 + b], t2_hi)
        t1_hi = jnp.where(sub == b, t1[8 + b], t1_hi)
    stair = [t1[0] + t2_lo, t1[0] + t2_hi]
    stair += [jnp.where(sub < TOPK // (a + 1), t1[a] + t2_lo, NEG_INF) for a in range(1, 8)]
    stair += [t1_hi + t2[0]]
    tau = _top16(stair)[TOPK - 1]
    top = t1[0] + t2[0]
    z = sum(jnp.where(cand >= tau, jnp.exp(cand - top), 0.0) for cand in stair)
    inv_z = 1.0 / _row_total(z)
    e2t_lo, e2t_hi = jnp.exp(t2_lo - t2[0]), jnp.exp(t2_hi - t2[0])
    inf = float("inf")
    thr = [_row_min(jnp.minimum(jnp.where(stair[0] >= tau, e2t_lo, inf), jnp.where(stair[1] >= tau, e2t_hi, inf)))]
    thr += [_row_min(jnp.where(cand >= tau, e2t_lo, inf)) for cand in stair[2:9]]
    e2t_top = jnp.exp(t2[0] - t2[0])
    thr += [jnp.where(t1[a] + t2[0] >= tau, e2t_top, inf) for a in range(8, TOPK)]
    thr1, c1, e2 = [], [], []
    for r in range(nb):
        c = jnp.full((SUBLANES, cols), inf, F32)
        for a in reversed(range(TOPK)):
            c = jnp.where(b1[r] == t1[a], thr[a], c)
        thr1.append(c)
        c1.append(jnp.exp(b1[r] - t1[0]) * (0.5 * inv_z))
        e2.append(jnp.exp(b2[r] - t2[0]))
    return thr1, c1, e2


def _gelu_tanh_x2(a):
    c = 0.7978845608028654
    t = jnp.tanh(a * (c + (c * 0.044715) * (a * a)))
    return a + a * t


PACK = 16
LANES = 128


PEER_STEPS = NEXP // (2 * EXPERT_ROWS)
GATE_BLOCKS = 8
VALUE_MATMUL_AFTER = (1, 4)


def _peer_kernel(h2tp_ref, h2ts_ref, wpq_ref, keys_ref, u_ref, vta_ref, vtb_ref, x1_ref, gt_ref, gfin_ref,
                 y_ref, ps_ref, thr1_ref, c1_ref, e2_ref, acc_ref, ga_ref, gb_ref, ht_ref, *, prompt_tiles):
    i = pl.program_id(0)
    s = pl.program_id(1)
    cols = ht_ref.shape[1]

    @pl.when(jnp.logical_and(s == 0, i < prompt_tiles))
    def _():
        ht_ref[...] = h2tp_ref[...]

    @pl.when(jnp.logical_and(s == 0, i >= prompt_tiles))
    def _():
        ht_ref[...] = h2ts_ref[...]

    ht = ht_ref[...]

    @pl.when(s == 0)
    def _():
        acc_ref[...] = jnp.zeros_like(acc_ref)
        gb_ref[...] = jnp.zeros_like(gb_ref)

        def head(h, carry):
            row0 = pl.multiple_of(h * 2 * NKEYS, 2 * NKEYS)
            qt = _dot(wpq_ref[pl.ds(row0, 2 * NKEYS), :], ht).astype(BF16)
            s1 = _dot(keys_ref[h], qt[0:NKEYS])
            s2 = _dot(keys_ref[PEER_HEADS + h], qt[NKEYS:2 * NKEYS])
            thr1, c1, e2 = _route_head(s1, s2)
            for r in range(NKEYS // SUBLANES):
                for cc in range(cols // LANES):
                    lanes = slice(cc * LANES, (cc + 1) * LANES)
                    thr1_ref[r, cc, h] = thr1[r][:, lanes]
                    c1_ref[r, cc, h] = c1[r][:, lanes]
                    e2_ref[h, cc, SUBLANES * r:SUBLANES * (r + 1), :] = e2[r][:, lanes]
            return carry

        lax.fori_loop(0, PEER_HEADS, head, 0)

    n_i1 = EXPERT_ROWS // NKEYS

    def build(tile, half, g_ref, vt_ref, g_done_ref):
        for j in range(n_i1):
            u0 = half * EXPERT_ROWS + j * NKEYS
            a = _dot(u_ref[u0:u0 + NKEYS, :], ht)
            if j in VALUE_MATMUL_AFTER:
                i = VALUE_MATMUL_AFTER.index(j)
                n = D_MODEL // len(VALUE_MATMUL_AFTER)
                acc_ref[i * n:(i + 1) * n, :] += _dot(vt_ref[0, i * n:(i + 1) * n, :], g_done_ref[...])
            for cc in range(cols // LANES):
                lanes = slice(cc * LANES, (cc + 1) * LANES)
                for kh in range(0, NKEYS // SUBLANES, GATE_BLOCKS):
                    w = [None] * GATE_BLOCKS
                    for h in range(PEER_HEADS):
                        thr = thr1_ref[tile, cc, h, j:j + 1, :]
                        c1 = c1_ref[tile, cc, h, j:j + 1, :]
                        for k in range(GATE_BLOCKS):
                            e2 = e2_ref[h, cc, (kh + k) * SUBLANES:(kh + k + 1) * SUBLANES, :]
                            term = jnp.where(e2 >= thr, e2, 0.0) * c1
                            w[k] = term if h == 0 else w[k] + term
                    for k in range(0, GATE_BLOCKS, 2):
                        r0 = (kh + k) * SUBLANES
                        gate = jnp.concatenate(w[k:k + 2], axis=0)
                        g_ref[j * NKEYS + r0:j * NKEYS + r0 + PACK, lanes] = (
                            gate * _gelu_tanh_x2(a[r0:r0 + PACK, lanes])).astype(BF16)

    @pl.when(s < PEER_STEPS)
    def _():
        build(2 * s, 0, ga_ref, vta_ref, gb_ref)
        build(2 * s + 1, 1, gb_ref, vtb_ref, ga_ref)

    @pl.when(s == PEER_STEPS)
    def _():
        acc_ref[...] += _dot(vta_ref[0], gb_ref[...])

    @pl.when(jnp.logical_and(s == PEER_STEPS, i < prompt_tiles))
    def _():
        y_ref[...] = _rms(x1_ref[...] + gt_ref[0] * acc_ref[...].T) * gfin_ref[...]

    @pl.when(jnp.logical_and(s == PEER_STEPS, i >= prompt_tiles))
    def _():
        ps_ref[...] = acc_ref[...].T


def _peer(h2t_p, h2t_s, x1_p, gt2_p, wpq, keys, u, vt, gfin, seq):
    tp = h2t_p.shape[1]
    cols = EXPERT_COLS
    assert h2t_s.shape[1] == cols and tp % cols == 0 and seq % cols == 0
    n_p = tp // cols
    last = NEXP // EXPERT_ROWS - 1
    p_tile = lambda i: jnp.minimum(i, n_p - 1)
    return pl.pallas_call(
        functools.partial(_peer_kernel, prompt_tiles=n_p),
        grid=(n_p + 1, PEER_STEPS + 1),
        in_specs=[pl.BlockSpec((D_MODEL, cols), lambda i, s: (0, p_tile(i))),
                  _const_spec((D_MODEL, cols)),
                  _const_spec((PEER_HEADS * 2 * NKEYS, D_MODEL)),
                  _const_spec((2 * PEER_HEADS, NKEYS, NKEYS)),
                  pl.BlockSpec((2 * EXPERT_ROWS, D_MODEL), lambda i, s: (jnp.minimum(s, PEER_STEPS - 1), 0)),
                  pl.BlockSpec((1, D_MODEL, EXPERT_ROWS), lambda i, s: (jnp.maximum(2 * s - 1, 0), 0, 0)),
                  pl.BlockSpec((1, D_MODEL, EXPERT_ROWS), lambda i, s: (jnp.minimum(2 * s, last), 0, 0)),
                  pl.BlockSpec((cols, D_MODEL), lambda i, s: (p_tile(i), 0)),
                  pl.BlockSpec((1, 1, D_MODEL), lambda i, s: (p_tile(i) // (seq // cols), 0, 0)),
                  _const_spec((1, D_MODEL))],
        out_specs=[pl.BlockSpec((cols, D_MODEL), lambda i, s: (p_tile(i), 0)), _const_spec((cols, D_MODEL))],
        out_shape=[jax.ShapeDtypeStruct((tp, D_MODEL), F32), jax.ShapeDtypeStruct((cols, D_MODEL), F32)],
        scratch_shapes=[pltpu.VMEM((NKEYS // SUBLANES, cols // LANES, PEER_HEADS, SUBLANES, LANES), F32),
                        pltpu.VMEM((NKEYS // SUBLANES, cols // LANES, PEER_HEADS, SUBLANES, LANES), F32),
                        pltpu.VMEM((PEER_HEADS, cols // LANES, NKEYS, LANES), F32),
                        pltpu.VMEM((D_MODEL, cols), F32),
                        pltpu.VMEM((EXPERT_ROWS, cols), BF16), pltpu.VMEM((EXPERT_ROWS, cols), BF16),
                        pltpu.VMEM((D_MODEL, cols), BF16)],
        compiler_params=_params("arbitrary", "arbitrary"),
        name="peer",
    )(h2t_p, h2t_s, wpq, keys, u, vt, vt, x1_p, gt2_p, gfin)


def _final_kernel(x1_ref, p_ref, gt_ref, gfin_ref, y_ref):
    y_ref[...] = _rms(x1_ref[...] + gt_ref[0] * p_ref[...]) * gfin_ref[...]


def _final(x1, p, gt, gfin, rows_per_mod):
    t = x1.shape[0]
    rows = POST_ROWS
    r = gt.shape[1]
    per = rows_per_mod // rows
    tok = pl.BlockSpec((rows, D_MODEL), lambda i: (i, 0))
    return pl.pallas_call(
        _final_kernel,
        grid=(t // rows,),
        in_specs=[tok, tok, pl.BlockSpec((1, r, D_MODEL), lambda i: (i // per, 0, 0)), _const_spec((1, D_MODEL))],
        out_specs=tok,
        out_shape=jax.ShapeDtypeStruct((t, D_MODEL), F32),
        compiler_params=_params("arbitrary"),
        name="final_norm",
    )(x1, p, gt, gfin)


def _group_matrix():
    g = np.arange(D_CONV) // (D_CONV // CONV_GROUPS)
    return jnp.asarray(g[:, None] == g[None, :], dtype=BF16)


def kernel(x_prompt, x_sample, c_prompt, c_sample, state_conv, state_mlstm_C, state_mlstm_n, state_mlstm_m,
           w_mod, b_mod, g_mix, w_in, w_conv, b_i, b_f, g_conv, g_mlstm, w_out, g_ffn, w_pq, sub_keys,
           u_tab, v_tab, g_final):
    depth = w_mod.shape[0]
    assert depth == 1
    nb, seq, _ = x_prompt.shape
    ns, sseq, _ = x_sample.shape
    ts = ns * sseq

    w = {
        "gmix": g_mix[0].reshape(1, -1),
        "win": w_in[0][:, :D_MAIN].astype(BF16),
        "wg": w_in[0][:, D_MAIN:].T,
        "wconv": w_conv[0],
        "bif": jnp.concatenate([b_i[0], b_f[0]]).reshape(1, 8),
        "gconv": g_conv[0].reshape(1, -1),
        "gmat": _group_matrix(),
        "gml": g_mlstm[0].reshape(1, -1),
        "wo": w_out[0].astype(BF16),
        "gffn": g_ffn[0].reshape(1, -1),
        "wpq": w_pq[0].T.astype(BF16),
        "keys": sub_keys[0].reshape(2 * PEER_HEADS, NKEYS, -1).astype(BF16),
        "u": u_tab[0].astype(BF16),
        "vt": v_tab[0].reshape(NEXP // EXPERT_ROWS, EXPERT_ROWS, D_MODEL).transpose(0, 2, 1).astype(BF16),
        "gfin": g_final.reshape(1, -1),
    }

    mod = _mod_rows(jnp.concatenate([c_prompt, c_sample], axis=0), w_mod[0], b_mod[0])
    mod_p = mod[:nb].reshape(nb, 1, 6, D_MODEL)
    sh1p, sc1p, gt1p, sh2p, sc2p, gt2p = (mod_p[:, :, i] for i in range(6))
    mod_s = jnp.broadcast_to(mod[nb:].reshape(ns, 1, 6, D_MODEL), (ns, sseq, 6, D_MODEL)).reshape(ts, 6, D_MODEL)
    sh1s, sc1s, gt1s, sh2s, sc2s, gt2s = (mod_s[:, i] for i in range(6))

    xp = x_prompt.reshape(nb * seq, D_MODEL)
    ycn, qkv, og, gates, tail = _premix_prompt(xp, sh1p, sc1p, w, nb, seq)
    hmn, c_p, n_p, m_p = _mlstm_prompt(qkv, og, gates, w["gml"], nb, seq)
    x1p, h2t_p = _postmix(xp, ycn, hmn, gt1p, sc2p, sh2p, w["wo"], w["gffn"], seq)
    conv_p = tail[:, 6:8][None]
    c_p = c_p[None]
    n_p = n_p[:, :, 0][None]
    m_p = m_p[:, :, 0, 0][None]

    xs = x_sample.reshape(ts, D_MODEL)
    st = state_conv[0]
    sm1 = jnp.pad(st[:, 1:2], ((0, 0), (0, sseq - 1), (0, 0))).reshape(ts, D_CONV)
    sm2 = jnp.pad(st, ((0, 0), (0, sseq - 2), (0, 0))).reshape(ts, D_CONV)
    ycn, qkv, og, gates, u_s = _premix_sample(xs, sh1s, sc1s, sm1, sm2, w, sseq)

    def heads(a):
        return a.astype(F32).reshape(ns, sseq, HEADS, DH).transpose(0, 2, 1, 3).reshape(ns * HEADS, sseq, DH)

    q, k, v = (heads(qkv[:, i * D_MLSTM:(i + 1) * D_MLSTM]) for i in range(3))
    gate_t = gates.reshape(ns, sseq, 2, HEADS).transpose(2, 0, 3, 1).reshape(2, ns * HEADS, 1, sseq)
    gml_g = jnp.tile(g_mlstm[0].reshape(HEADS, 1, DH), (ns, 1, 1))
    hm, c_s, n_s, m_s = _mlstm_sample(
        q, q.transpose(0, 2, 1), k, k.transpose(0, 2, 1), v, heads(og), gate_t[0], gate_t[1], gml_g,
        state_mlstm_C[0].reshape(ns * HEADS, DH, DH), state_mlstm_n[0].reshape(ns * HEADS, 1, DH),
        state_mlstm_m[0].reshape(ns * HEADS, 1, 1), sseq)
    hmn = hm.reshape(ns, HEADS, sseq, DH).transpose(0, 2, 1, 3).reshape(ts, D_MLSTM).astype(BF16)
    per_tok = lambda a: a.reshape(ts // POST_ROWS, POST_ROWS, D_MODEL)
    x1s, h2t_s = _postmix(xs, ycn, hmn, per_tok(gt1s), per_tok(sc2s), per_tok(sh2s), w["wo"], w["gffn"], POST_ROWS)

    y_p, p_s = _peer(h2t_p, h2t_s, x1p, gt2p, w["wpq"], w["keys"], w["u"], w["vt"], w["gfin"], seq)
    y_prompt = y_p.reshape(nb, seq, D_MODEL)
    y_sample = _final(x1s, p_s, per_tok(gt2s), w["gfin"], POST_ROWS).reshape(ns, sseq, D_MODEL)
    conv_s = u_s.reshape(ns, sseq, D_CONV)[:, sseq - 2:][None]
    c_s = c_s.reshape(ns, HEADS, DH, DH)[None]
    n_s = n_s.reshape(ns, HEADS, DH)[None]
    m_s = m_s.reshape(ns, HEADS)[None]

    return (y_prompt, y_sample, conv_p, c_p, n_p, m_p, conv_s, c_s, n_s, m_s)
```

```python
import functools

import jax
import jax.numpy as jnp
import numpy as np
from jax import lax
from jax.experimental import pallas as pl
from jax.experimental.pallas import tpu as pltpu

F32 = jnp.float32
BF16 = jnp.bfloat16
NEG_INF = float("-inf")
EPS = 1e-6

D_MODEL = 1024
D_CONV = 512
CONV_GROUPS = 8
D_MLSTM = 512
HEADS = 4
DH = 128
D_MAIN = 3 * D_CONV + 4 * D_MLSTM
PEER_HEADS = 8
NKEYS = 128
TOPK = 16
NEXP = NKEYS * NKEYS

VMEM_LIMIT = 56 * 1024 * 1024
MOD_SH1, MOD_SC1, MOD_GT1, MOD_SH2, MOD_SC2, MOD_GT2 = range(6)

MOD_ROWS = 128
PREMIX_ROWS = 256
CHUNK = 256
MLSTM_SEQS = 1
POST_ROWS = 512
EXPERT_COLS = 512
EXPERT_ROWS = 1024
SAMPLE_GROUP = 32


def _dot(a, b):
    return jnp.dot(a, b, preferred_element_type=F32)


def _dot_nt(a, b):
    return lax.dot_general(a, b, (((1,), (1,)), ((), ())), preferred_element_type=F32)


def _dot_tn(a, b):
    return lax.dot_general(a, b, (((0,), (0,)), ((), ())), preferred_element_type=F32)


def _dot_f32(a, b):
    return jnp.dot(a, b, precision=lax.Precision.HIGHEST, preferred_element_type=F32)


def _params(*sem):
    return pltpu.CompilerParams(dimension_semantics=sem, vmem_limit_bytes=VMEM_LIMIT)


def _rms(x):
    return x * lax.rsqrt(jnp.mean(x * x, axis=-1, keepdims=True) + EPS)


def _mod_kernel(c_ref, w_ref, b_ref, o_ref):
    @pl.when(pl.program_id(0) == 0)
    def _():
        o_ref[...] = jnp.broadcast_to(b_ref[...], o_ref.shape)

    c = c_ref[...]
    a = (c * jax.nn.sigmoid(c)).astype(BF16)
    o_ref[...] += _dot(a, w_ref[...].astype(BF16))


def _mod_rows(c, w_mod, b_mod):
    n, width = c.shape[0], w_mod.shape[1]
    tk = MOD_ROWS
    return pl.pallas_call(
        _mod_kernel,
        grid=(D_MODEL // tk,),
        in_specs=[pl.BlockSpec((n, tk), lambda k: (0, k)),
                  pl.BlockSpec((tk, width), lambda k: (k, 0)),
                  _const_spec((1, width))],
        out_specs=_const_spec((n, width)),
        out_shape=jax.ShapeDtypeStruct((n, width), F32),
        compiler_params=_params("arbitrary"),
        name="mod_rows",
    )(c, w_mod, b_mod.reshape(1, -1))


def _log_sigmoid(x):
    return jnp.minimum(x, 0.0) - jnp.log1p(jnp.exp(-jnp.abs(x)))


def _premix_core(x, sh, sc, gmix, win_ref, wg_ref, bif):
    h = _rms(x) * gmix * (1.0 + sc) + sh
    z = _dot(h.astype(BF16), win_ref[...])
    lane = lax.broadcasted_iota(jnp.int32, (x.shape[0], 2 * HEADS), 1)
    pre = bif
    for j in range(2 * HEADS):
        col = jnp.sum(h * wg_ref[j:j + 1, :], axis=1, keepdims=True)
        pre = pre + jnp.where(lane == j, col, 0.0)
    gates = jnp.where(lane < HEADS, pre, _log_sigmoid(pre))
    return z, gates


def _conv_tail(z, um1, um2, u, wconv, gconv, gmat_ref, ycn_ref, qkv_ref, og_ref):
    bg = z[:, D_CONV:2 * D_CONV]
    conv = wconv[0:1] * um2 + wconv[1:2] * um1 + wconv[2:3] * u
    yc = bg * conv
    ysq = yc * yc
    hi = ysq.astype(BF16)
    lo = (ysq - hi.astype(F32)).astype(BF16)
    gsum = _dot(hi, gmat_ref[...]) + _dot(lo, gmat_ref[...])
    ycn_ref[...] = (yc * lax.rsqrt(gsum * (CONV_GROUPS / D_CONV) + EPS) * gconv).astype(BF16)
    o0 = 3 * D_CONV
    qkv_ref[:, 0:D_MLSTM] = z[:, o0:o0 + D_MLSTM].astype(BF16)
    qkv_ref[:, D_MLSTM:2 * D_MLSTM] = (z[:, o0 + D_MLSTM:o0 + 2 * D_MLSTM] * (DH ** -0.5)).astype(BF16)
    qkv_ref[:, 2 * D_MLSTM:3 * D_MLSTM] = z[:, o0 + 2 * D_MLSTM:o0 + 3 * D_MLSTM].astype(BF16)
    og_ref[...] = jax.nn.sigmoid(z[:, o0 + 3 * D_MLSTM:o0 + 4 * D_MLSTM]).astype(BF16)


def _premix_prompt_kernel(x_ref, sh_ref, sc_ref, gmix_ref, win_ref, wg_ref, wconv_ref, bif_ref, gconv_ref,
                          gmat_ref, ycn_ref, qkv_ref, og_ref, gate_ref, tail_ref, carry_ref):
    rows = x_ref.shape[0]

    @pl.when(pl.program_id(1) == 0)
    def _():
        carry_ref[...] = jnp.zeros_like(carry_ref)

    z, gates = _premix_core(x_ref[...], sh_ref[0], sc_ref[0], gmix_ref[...], win_ref, wg_ref, bif_ref[...])
    gate_ref[...] = gates
    u = z[:, 2 * D_CONV:3 * D_CONV] * z[:, 0:D_CONV]
    prev = carry_ref[...]
    p0, p1 = prev[6:7], prev[7:8]
    ri = lax.broadcasted_iota(jnp.int32, (rows, 1), 0)
    um1 = jnp.where(ri == 0, p1, pltpu.roll(u, 1, 0))
    um2 = jnp.where(ri == 0, p0, jnp.where(ri == 1, p1, pltpu.roll(u, 2, 0)))
    carry_ref[...] = u[rows - 8:rows]
    tail_ref[0] = u[rows - 8:rows]
    _conv_tail(z, um1, um2, u, wconv_ref[...], gconv_ref[...], gmat_ref, ycn_ref, qkv_ref, og_ref)


def _premix_sample_kernel(x_ref, sh_ref, sc_ref, gmix_ref, win_ref, wg_ref, wconv_ref, bif_ref, gconv_ref,
                          gmat_ref, sm1_ref, sm2_ref, ycn_ref, qkv_ref, og_ref, gate_ref, u_ref, *, seq):
    rows = x_ref.shape[0]
    z, gates = _premix_core(x_ref[...], sh_ref[...], sc_ref[...], gmix_ref[...], win_ref, wg_ref, bif_ref[...])
    gate_ref[...] = gates
    u = z[:, 2 * D_CONV:3 * D_CONV] * z[:, 0:D_CONV]
    u_ref[...] = u
    tmod = lax.broadcasted_iota(jnp.int32, (rows, 1), 0) % seq
    um1 = jnp.where(tmod == 0, sm1_ref[...], pltpu.roll(u, 1, 0))
    um2 = jnp.where(tmod < 2, sm2_ref[...], pltpu.roll(u, 2, 0))
    _conv_tail(z, um1, um2, u, wconv_ref[...], gconv_ref[...], gmat_ref, ycn_ref, qkv_ref, og_ref)


def _const_spec(shape):
    nd = len(shape)
    return pl.BlockSpec(shape, lambda *_: (0,) * nd)


def _premix_weights(w):
    return [w["gmix"], w["win"], w["wg"], w["wconv"], w["bif"], w["gconv"], w["gmat"]]


def _premix_weight_specs():
    return [_const_spec((1, D_MODEL)), _const_spec((D_MODEL, D_MAIN)), _const_spec((8, D_MODEL)),
            _const_spec((3, D_CONV)), _const_spec((1, 8)), _const_spec((1, D_CONV)),
            _const_spec((D_CONV, D_CONV))]


def _premix_prompt(x, mod, w, nb, seq):
    t = x.shape[0]
    rows = PREMIX_ROWS
    nl = seq // rows
    tok = lambda b, l: (b * nl + l, 0)
    per_seq = lambda col: pl.BlockSpec((1, 1, D_MODEL), lambda b, l: (b, 0, col))
    return pl.pallas_call(
        _premix_prompt_kernel,
        grid=(nb, nl),
        in_specs=[pl.BlockSpec((rows, D_MODEL), tok), per_seq(MOD_SH1), per_seq(MOD_SC1)] + _premix_weight_specs(),
        out_specs=[pl.BlockSpec((rows, D_CONV), tok), pl.BlockSpec((rows, 3 * D_MLSTM), tok),
                   pl.BlockSpec((rows, D_MLSTM), tok), pl.BlockSpec((rows, 8), tok),
                   pl.BlockSpec((1, 8, D_CONV), lambda b, l: (b, 0, 0))],
        out_shape=[jax.ShapeDtypeStruct((t, D_CONV), BF16), jax.ShapeDtypeStruct((t, 3 * D_MLSTM), BF16),
                   jax.ShapeDtypeStruct((t, D_MLSTM), BF16), jax.ShapeDtypeStruct((t, 8), F32),
                   jax.ShapeDtypeStruct((nb, 8, D_CONV), F32)],
        scratch_shapes=[pltpu.VMEM((8, D_CONV), F32)],
        compiler_params=_params("arbitrary", "arbitrary"),
        name="premix_prompt",
    )(x, mod, mod, *_premix_weights(w))


def _premix_sample(x, mod, sm1, sm2, w, seq):
    t = x.shape[0]
    full = lambda n: _const_spec((t, n))
    return pl.pallas_call(
        functools.partial(_premix_sample_kernel, seq=seq),
        grid=(1,),
        in_specs=[full(D_MODEL), pl.BlockSpec((t, D_MODEL), lambda i: (0, MOD_SH1)),
                  pl.BlockSpec((t, D_MODEL), lambda i: (0, MOD_SC1))] + _premix_weight_specs()
        + [full(D_CONV), full(D_CONV)],
        out_specs=[full(D_CONV), full(3 * D_MLSTM), full(D_MLSTM), full(8), full(D_CONV)],
        out_shape=[jax.ShapeDtypeStruct((t, D_CONV), BF16), jax.ShapeDtypeStruct((t, 3 * D_MLSTM), BF16),
                   jax.ShapeDtypeStruct((t, D_MLSTM), BF16), jax.ShapeDtypeStruct((t, 8), F32),
                   jax.ShapeDtypeStruct((t, D_CONV), F32)],
        compiler_params=_params("arbitrary"),
        name="premix_sample",
    )(x, mod, mod, *_premix_weights(w), sm1, sm2)


def _mlstm_chunk_kernel(qkv_ref, og_ref, gate_ref, gml_ref, h_ref, c_ref, n_ref, m_ref, c_scr, n_scr, m_scr):
    nseq, lc = qkv_ref.shape[0], qkv_ref.shape[1]

    @pl.when(pl.program_id(1) == 0)
    def _():
        c_scr[...] = jnp.zeros_like(c_scr)
        n_scr[...] = jnp.zeros_like(n_scr)
        m_scr[...] = jnp.zeros_like(m_scr)

    eye = (lax.broadcasted_iota(jnp.int32, (8, 8), 0) == lax.broadcasted_iota(jnp.int32, (8, 8), 1)).astype(F32)
    r = lax.broadcasted_iota(jnp.int32, (lc, lc), 0)
    c = lax.broadcasted_iota(jnp.int32, (lc, lc), 1)
    causal = c <= r
    tril = causal.astype(F32)

    for bb in range(nseq):
        gates = gate_ref[bb]
        grow = lax.dot_general(eye, gates, (((1,), (1,)), ((), ())), precision=lax.Precision.HIGHEST,
                               preferred_element_type=F32)
        bcol = _dot_f32(tril, gates)
        brow = lax.dot_general(grow, tril, (((1,), (1,)), ((), ())), precision=lax.Precision.HIGHEST,
                               preferred_element_type=F32)
        for h in range(HEADS):
            st = bb * HEADS + h
            q = qkv_ref[bb, :, h * DH:(h + 1) * DH]
            k = qkv_ref[bb, :, D_MLSTM + h * DH:D_MLSTM + (h + 1) * DH]
            v = qkv_ref[bb, :, 2 * D_MLSTM + h * DH:2 * D_MLSTM + (h + 1) * DH]
            li_c = gates[:, h:h + 1]
            li_r = grow[h:h + 1, :]
            b_c = bcol[:, HEADS + h:HEADS + h + 1]
            b_r = brow[HEADS + h:HEADS + h + 1, :]
            m_old = m_scr[st][0:1, 0:1]
            c_old = c_scr[st]
            n_old = n_scr[st][0:1, :]

            a = b_c + m_old
            dmat = jnp.where(causal, b_c - b_r + li_r, NEG_INF)
            mt = jnp.maximum(a, jnp.max(dmat, axis=1, keepdims=True))
            s = _dot_nt(q, k) * jnp.exp(dmat - mt)
            w_inter = jnp.exp(a - mt)
            num = w_inter * _dot(q, c_old.astype(BF16)) + _dot(s.astype(BF16), v)
            den = (w_inter * jnp.sum(q.astype(F32) * n_old, axis=1, keepdims=True)
                   + jnp.sum(s, axis=1, keepdims=True))
            hh = num / jnp.maximum(jnp.abs(den), jnp.exp(-mt))
            hh = og_ref[bb, :, h * DH:(h + 1) * DH].astype(F32) * hh
            h_ref[bb, :, h * DH:(h + 1) * DH] = (_rms(hh) * gml_ref[:, h * DH:(h + 1) * DH]).astype(BF16)

            bl = b_c[lc - 1:lc, :]
            gl = bl - b_c + li_c
            m_new = jnp.maximum(bl + m_old, jnp.max(gl, axis=0, keepdims=True))
            w_old = jnp.exp(bl + m_old - m_new)
            kw = k.astype(F32) * jnp.exp(gl - m_new)
            c_new = w_old * c_old + _dot_tn(kw.astype(BF16), v)
            n_new = w_old * n_old + jnp.sum(kw, axis=0, keepdims=True)
            c_scr[st] = c_new
            n_scr[st] = jnp.broadcast_to(n_new, (8, DH))
            m_scr[st] = jnp.broadcast_to(m_new, (8, DH))
            c_ref[bb, h] = c_new
            n_ref[bb, h] = jnp.broadcast_to(n_new, (8, DH))
            m_ref[bb, h] = jnp.broadcast_to(m_new, (8, DH))


def _mlstm_prompt(qkv, og, gates, gml, nb, seq):
    nc = seq // CHUNK
    ns = MLSTM_SEQS
    tok = lambda b, c: (b, c, 0)
    state = lambda b, c: (b, 0, 0, 0)
    per_seq = lambda a: a.reshape(nb, seq, a.shape[-1])
    hmn, c_p, n_p, m_p = pl.pallas_call(
        _mlstm_chunk_kernel,
        grid=(nb // ns, nc),
        in_specs=[pl.BlockSpec((ns, CHUNK, 3 * D_MLSTM), tok), pl.BlockSpec((ns, CHUNK, D_MLSTM), tok),
                  pl.BlockSpec((ns, CHUNK, 8), tok), _const_spec((1, D_MLSTM))],
        out_specs=[pl.BlockSpec((ns, CHUNK, D_MLSTM), tok), pl.BlockSpec((ns, HEADS, DH, DH), state),
                   pl.BlockSpec((ns, HEADS, 8, DH), state), pl.BlockSpec((ns, HEADS, 8, DH), state)],
        out_shape=[jax.ShapeDtypeStruct((nb, seq, D_MLSTM), BF16), jax.ShapeDtypeStruct((nb, HEADS, DH, DH), F32),
                   jax.ShapeDtypeStruct((nb, HEADS, 8, DH), F32), jax.ShapeDtypeStruct((nb, HEADS, 8, DH), F32)],
        scratch_shapes=[pltpu.VMEM((ns * HEADS, DH, DH), F32), pltpu.VMEM((ns * HEADS, 8, DH), F32),
                        pltpu.VMEM((ns * HEADS, 8, DH), F32)],
        compiler_params=_params("arbitrary", "arbitrary"),
        name="mlstm_prompt",
    )(per_seq(qkv), per_seq(og), per_seq(gates), gml)
    return hmn.reshape(nb * seq, D_MLSTM), c_p, n_p, m_p


def _mlstm_step_kernel(q_ref, k_ref, v_ref, og_ref, li_ref, lf_ref, gml_ref, c0_ref, n0_ref,
                       m0_ref, h_ref, c_ref, n_ref, m_ref, *, seq):
    qt_all = jnp.swapaxes(q_ref[...], 1, 2)
    kt_all = jnp.swapaxes(k_ref[...], 1, 2)
    c = c0_ref[...]
    n = n0_ref[...]
    m = m0_ref[...]
    for t in range(seq):
        lit = li_ref[:, :, t:t + 1]
        lft = lf_ref[:, :, t:t + 1]
        m_new = jnp.maximum(lft + m, lit)
        fw = jnp.exp(lft + m - m_new)
        iw = jnp.exp(lit - m_new)
        c = fw * c + (iw * kt_all[:, :, t:t + 1]) * v_ref[:, t:t + 1, :]
        n = fw * n + iw * k_ref[:, t:t + 1, :]
        num = jnp.sum(qt_all[:, :, t:t + 1] * c, axis=1, keepdims=True)
        den = jnp.sum(q_ref[:, t:t + 1, :] * n, axis=2, keepdims=True)
        hh = og_ref[:, t:t + 1, :] * (num / jnp.maximum(jnp.abs(den), jnp.exp(-m_new)))
        h_ref[:, t:t + 1, :] = _rms(hh) * gml_ref[...]
        m = m_new
    c_ref[...] = c
    n_ref[...] = n
    m_ref[...] = m


def _mlstm_sample(q, k, v, og, li, lf, gml, c0, n0, m0, seq):
    ng = q.shape[0]
    g = SAMPLE_GROUP
    blk = lambda *shape: pl.BlockSpec((g,) + shape, lambda i: (i,) + (0,) * len(shape))
    return pl.pallas_call(
        functools.partial(_mlstm_step_kernel, seq=seq),
        grid=(ng // g,),
        in_specs=[blk(seq, DH), blk(seq, DH), blk(seq, DH), blk(seq, DH),
                  blk(1, seq), blk(1, seq), blk(1, DH), blk(DH, DH), blk(1, DH), blk(1, 1)],
        out_specs=[blk(seq, DH), blk(DH, DH), blk(1, DH), blk(1, 1)],
        out_shape=[jax.ShapeDtypeStruct((ng, seq, DH), F32), jax.ShapeDtypeStruct((ng, DH, DH), F32),
                   jax.ShapeDtypeStruct((ng, 1, DH), F32), jax.ShapeDtypeStruct((ng, 1, 1), F32)],
        compiler_params=_params("arbitrary"),
        name="mlstm_sample",
    )(q, k, v, og, li, lf, gml, c0, n0, m0)


def _postmix_kernel(x_ref, ycn_ref, hmn_ref, gt_ref, sc_ref, sh_ref, wo_ref, gffn_ref, x1_ref, h2t_ref):
    y = _dot(ycn_ref[...], wo_ref[0:D_CONV, :]) + _dot(hmn_ref[...], wo_ref[D_CONV:D_CONV + D_MLSTM, :])
    x1 = x_ref[...] + gt_ref[0] * y
    x1_ref[...] = x1
    h2 = _rms(x1) * gffn_ref[...] * (1.0 + sc_ref[0]) + sh_ref[0]
    h2t_ref[...] = h2.T.astype(BF16)


def _postmix(x, ycn, hmn, mod, wo, gffn, rows_per_mod):
    t = x.shape[0]
    rows = POST_ROWS
    r = mod.shape[1]
    per = rows_per_mod // rows
    tok = lambda i: (i, 0)
    part = lambda col: pl.BlockSpec((1, r, D_MODEL), lambda i: (i // per, 0, col))
    return pl.pallas_call(
        _postmix_kernel,
        grid=(t // rows,),
        in_specs=[pl.BlockSpec((rows, D_MODEL), tok), pl.BlockSpec((rows, D_CONV), tok),
                  pl.BlockSpec((rows, D_MLSTM), tok), part(MOD_GT1), part(MOD_SC2), part(MOD_SH2),
                  _const_spec((D_MODEL, D_MODEL)), _const_spec((1, D_MODEL))],
        out_specs=[pl.BlockSpec((rows, D_MODEL), tok), pl.BlockSpec((D_MODEL, rows), lambda i: (0, i))],
        out_shape=[jax.ShapeDtypeStruct((t, D_MODEL), F32), jax.ShapeDtypeStruct((D_MODEL, t), BF16)],
        compiler_params=_params("arbitrary"),
        name="postmix",
    )(x, ycn, hmn, mod, mod, mod, wo, gffn)


def _odd_even_merge_sort_pairs(n):
    pairs = []

    def merge(lo, m, r):
        step = r * 2
        if step < m:
            merge(lo, m, step)
            merge(lo + r, m, step)
            pairs.extend((i, i + r) for i in range(lo + r, lo + m - r, step))
        else:
            pairs.append((lo, lo + r))

    def sort(lo, m):
        if m > 1:
            sort(lo, m // 2)
            sort(lo + m // 2, m // 2)
            merge(lo, m, 1)

    sort(0, n)
    return pairs


_SORT16 = _odd_even_merge_sort_pairs(TOPK)
_BITONIC16 = [(i, i + d) for d in (8, 4, 2, 1) for i in range(TOPK) if not i & d]
SUBLANES = 8


def _exchange(x, pairs):
    for i, j in pairs:
        if x[j] is None:
            continue
        if x[i] is None:
            x[i], x[j] = x[j], None
        else:
            x[i], x[j] = jnp.maximum(x[i], x[j]), jnp.minimum(x[i], x[j])


def _top16(blocks):
    x = list(blocks) + [None] * (TOPK - len(blocks))
    _exchange(x, _SORT16)
    for d in (4, 2, 1):
        p = [None if b is None else pltpu.roll(b, d, 0) for b in x]
        merged = []
        for r in range(TOPK):
            a, b = x[r], p[TOPK - 1 - r]
            merged.append(b if a is None else a if b is None else jnp.maximum(a, b))
        x = merged
        _exchange(x, _BITONIC16)
    return x


def _row_total(x):
    for d in (4, 2, 1):
        x = x + pltpu.roll(x, d, 0)
    return x


def _row_min(x):
    for d in (4, 2, 1):
        x = jnp.minimum(x, pltpu.roll(x, d, 0))
    return x


def _route_head(s1, s2):
    cols = s1.shape[1]
    nb = NKEYS // SUBLANES
    b1 = [s1[SUBLANES * r:SUBLANES * (r + 1)] for r in range(nb)]
    b2 = [s2[SUBLANES * r:SUBLANES * (r + 1)] for r in range(nb)]
    t1 = _top16(b1)
    t2 = _top16(b2)
    sub = lax.broadcasted_iota(jnp.int32, (SUBLANES, cols), 0)
    t2_lo, t2_hi, t1_hi = t2[0], t2[8], t1[8]
    for b in range(1, SUBLANES):
        t2_lo = jnp.where(sub == b, t2[b], t2_lo)
        t2_hi = jnp.where(sub == b, t2[8 + b], t2_hi)
        t1_hi = jnp.where(sub == b, t1[8 + b], t1_hi)
    stair = [t1[0] + t2_lo, t1[0] + t2_hi]
    stair += [jnp.where(sub < TOPK // (a + 1), t1[a] + t2_lo, NEG_INF) for a in range(1, 8)]
    stair += [t1_hi + t2[0]]
    tau = _top16(stair)[TOPK - 1]
    top = t1[0] + t2[0]
    z = sum(jnp.where(cand >= tau, jnp.exp(cand - top), 0.0) for cand in stair)
    inv_z = 1.0 / _row_total(z)
    e2t_lo, e2t_hi = jnp.exp(t2_lo - t2[0]), jnp.exp(t2_hi - t2[0])
    inf = float("inf")
    thr = [_row_min(jnp.minimum(jnp.where(stair[0] >= tau, e2t_lo, inf), jnp.where(stair[1] >= tau, e2t_hi, inf)))]
    thr += [_row_min(jnp.where(cand >= tau, e2t_lo, inf)) for cand in stair[2:9]]
    e2t_top = jnp.exp(t2[0] - t2[0])
    thr += [jnp.where(t1[a] + t2[0] >= tau, e2t_top, inf) for a in range(8, TOPK)]
    thr1, c1, e2 = [], [], []
    for r in range(nb):
        c = jnp.full((SUBLANES, cols), inf, F32)
        for a in reversed(range(TOPK)):
            c = jnp.where(b1[r] == t1[a], thr[a], c)
        thr1.append(c)
        c1.append(jnp.exp(b1[r] - t1[0]) * (0.5 * inv_z))
        e2.append(jnp.exp(b2[r] - t2[0]))
    return thr1, c1, e2


def _gelu_tanh_x2(a):
    c = 0.7978845608028654
    t = jnp.tanh(a * (c + (c * 0.044715) * (a * a)))
    return a + a * t


PACK = 16
LANES = 128


PEER_STEPS = NEXP // (2 * EXPERT_ROWS)
GATE_BLOCKS = 8
VALUE_MATMUL_AFTER = (1, 4)


def _peer_kernel(h2tp_ref, h2ts_ref, wpq_ref, keys_ref, u_ref, vta_ref, vtb_ref, x1_ref, gt_ref, gfin_ref,
                 y_ref, ps_ref, thr1_ref, c1_ref, e2_ref, acc_ref, ga_ref, gb_ref, ht_ref, *, prompt_tiles):
    i = pl.program_id(0)
    s = pl.program_id(1)
    cols = ht_ref.shape[1]

    @pl.when(jnp.logical_and(s == 0, i < prompt_tiles))
    def _():
        ht_ref[...] = h2tp_ref[...]

    @pl.when(jnp.logical_and(s == 0, i >= prompt_tiles))
    def _():
        ht_ref[...] = h2ts_ref[...]

    ht = ht_ref[...]

    @pl.when(s == 0)
    def _():
        acc_ref[...] = jnp.zeros_like(acc_ref)
        gb_ref[...] = jnp.zeros_like(gb_ref)

        def head(h, carry):
            row0 = pl.multiple_of(h * 2 * NKEYS, 2 * NKEYS)
            qt = _dot(wpq_ref[pl.ds(row0, 2 * NKEYS), :], ht).astype(BF16)
            s1 = _dot(keys_ref[h], qt[0:NKEYS])
            s2 = _dot(keys_ref[PEER_HEADS + h], qt[NKEYS:2 * NKEYS])
            thr1, c1, e2 = _route_head(s1, s2)
            for r in range(NKEYS // SUBLANES):
                for cc in range(cols // LANES):
                    lanes = slice(cc * LANES, (cc + 1) * LANES)
                    thr1_ref[r, cc, h] = thr1[r][:, lanes]
                    c1_ref[r, cc, h] = c1[r][:, lanes]
                    e2_ref[h, cc, SUBLANES * r:SUBLANES * (r + 1), :] = e2[r][:, lanes]
            return carry

        lax.fori_loop(0, PEER_HEADS, head, 0)

    n_i1 = EXPERT_ROWS // NKEYS

    def build(tile, half, g_ref, vt_ref, g_done_ref):
        for j in range(n_i1):
            u0 = half * EXPERT_ROWS + j * NKEYS
            a = _dot(u_ref[u0:u0 + NKEYS, :], ht)
            if j in VALUE_MATMUL_AFTER:
                i = VALUE_MATMUL_AFTER.index(j)
                n = D_MODEL // len(VALUE_MATMUL_AFTER)
                acc_ref[i * n:(i + 1) * n, :] += _dot(vt_ref[0, i * n:(i + 1) * n, :], g_done_ref[...])
            for cc in range(cols // LANES):
                lanes = slice(cc * LANES, (cc + 1) * LANES)
                for kh in range(0, NKEYS // SUBLANES, GATE_BLOCKS):
                    w = [None] * GATE_BLOCKS
                    for h in range(PEER_HEADS):
                        thr = thr1_ref[tile, cc, h, j:j + 1, :]
                        c1 = c1_ref[tile, cc, h, j:j + 1, :]
                        for k in range(GATE_BLOCKS):
                            e2 = e2_ref[h, cc, (kh + k) * SUBLANES:(kh + k + 1) * SUBLANES, :]
                            term = jnp.where(e2 >= thr, e2, 0.0) * c1
                            w[k] = term if h == 0 else w[k] + term
                    for k in range(0, GATE_BLOCKS, 2):
                        r0 = (kh + k) * SUBLANES
                        gate = jnp.concatenate(w[k:k + 2], axis=0)
                        g_ref[j * NKEYS + r0:j * NKEYS + r0 + PACK, lanes] = (
                            gate * _gelu_tanh_x2(a[r0:r0 + PACK, lanes])).astype(BF16)

    @pl.when(s < PEER_STEPS)
    def _():
        build(2 * s, 0, ga_ref, vta_ref, gb_ref)
        build(2 * s + 1, 1, gb_ref, vtb_ref, ga_ref)

    @pl.when(s == PEER_STEPS)
    def _():
        acc_ref[...] += _dot(vta_ref[0], gb_ref[...])

    @pl.when(jnp.logical_and(s == PEER_STEPS, i < prompt_tiles))
    def _():
        y_ref[...] = _rms(x1_ref[...] + gt_ref[0] * acc_ref[...].T) * gfin_ref[...]

    @pl.when(jnp.logical_and(s == PEER_STEPS, i >= prompt_tiles))
    def _():
        ps_ref[...] = acc_ref[...].T


def _peer(h2t_p, h2t_s, x1_p, mod_p, wpq, keys, u, vt, gfin, seq):
    tp = h2t_p.shape[1]
    cols = EXPERT_COLS
    assert h2t_s.shape[1] == cols and tp % cols == 0 and seq % cols == 0
    n_p = tp // cols
    last = NEXP // EXPERT_ROWS - 1
    p_tile = lambda i: jnp.minimum(i, n_p - 1)
    return pl.pallas_call(
        functools.partial(_peer_kernel, prompt_tiles=n_p),
        grid=(n_p + 1, PEER_STEPS + 1),
        in_specs=[pl.BlockSpec((D_MODEL, cols), lambda i, s: (0, p_tile(i))),
                  _const_spec((D_MODEL, cols)),
                  _const_spec((PEER_HEADS * 2 * NKEYS, D_MODEL)),
                  _const_spec((2 * PEER_HEADS, NKEYS, NKEYS)),
                  pl.BlockSpec((2 * EXPERT_ROWS, D_MODEL), lambda i, s: (jnp.minimum(s, PEER_STEPS - 1), 0)),
                  pl.BlockSpec((1, D_MODEL, EXPERT_ROWS), lambda i, s: (jnp.maximum(2 * s - 1, 0), 0, 0)),
                  pl.BlockSpec((1, D_MODEL, EXPERT_ROWS), lambda i, s: (jnp.minimum(2 * s, last), 0, 0)),
                  pl.BlockSpec((cols, D_MODEL), lambda i, s: (p_tile(i), 0)),
                  pl.BlockSpec((1, 1, D_MODEL), lambda i, s: (p_tile(i) // (seq // cols), 0, MOD_GT2)),
                  _const_spec((1, D_MODEL))],
        out_specs=[pl.BlockSpec((cols, D_MODEL), lambda i, s: (p_tile(i), 0)), _const_spec((cols, D_MODEL))],
        out_shape=[jax.ShapeDtypeStruct((tp, D_MODEL), F32), jax.ShapeDtypeStruct((cols, D_MODEL), F32)],
        scratch_shapes=[pltpu.VMEM((NKEYS // SUBLANES, cols // LANES, PEER_HEADS, SUBLANES, LANES), F32),
                        pltpu.VMEM((NKEYS // SUBLANES, cols // LANES, PEER_HEADS, SUBLANES, LANES), F32),
                        pltpu.VMEM((PEER_HEADS, cols // LANES, NKEYS, LANES), F32),
                        pltpu.VMEM((D_MODEL, cols), F32),
                        pltpu.VMEM((EXPERT_ROWS, cols), BF16), pltpu.VMEM((EXPERT_ROWS, cols), BF16),
                        pltpu.VMEM((D_MODEL, cols), BF16)],
        compiler_params=_params("arbitrary", "arbitrary"),
        name="peer",
    )(h2t_p, h2t_s, wpq, keys, u, vt, vt, x1_p, mod_p, gfin)


def _final_kernel(x1_ref, p_ref, gt_ref, gfin_ref, y_ref):
    y_ref[...] = _rms(x1_ref[...] + gt_ref[0] * p_ref[...]) * gfin_ref[...]


def _final(x1, p, mod, gfin, rows_per_mod):
    t = x1.shape[0]
    rows = POST_ROWS
    r = mod.shape[1]
    per = rows_per_mod // rows
    tok = pl.BlockSpec((rows, D_MODEL), lambda i: (i, 0))
    return pl.pallas_call(
        _final_kernel,
        grid=(t // rows,),
        in_specs=[tok, tok, pl.BlockSpec((1, r, D_MODEL), lambda i: (i // per, 0, MOD_GT2)),
                  _const_spec((1, D_MODEL))],
        out_specs=tok,
        out_shape=jax.ShapeDtypeStruct((t, D_MODEL), F32),
        compiler_params=_params("arbitrary"),
        name="final_norm",
    )(x1, p, mod, gfin)


def _group_matrix():
    g = np.arange(D_CONV) // (D_CONV // CONV_GROUPS)
    return jnp.asarray(g[:, None] == g[None, :], dtype=BF16)


def kernel(x_prompt, x_sample, c_prompt, c_sample, state_conv, state_mlstm_C, state_mlstm_n, state_mlstm_m,
           w_mod, b_mod, g_mix, w_in, w_conv, b_i, b_f, g_conv, g_mlstm, w_out, g_ffn, w_pq, sub_keys,
           u_tab, v_tab, g_final):
    depth = w_mod.shape[0]
    assert depth == 1
    nb, seq, _ = x_prompt.shape
    ns, sseq, _ = x_sample.shape
    ts = ns * sseq

    w = {
        "gmix": g_mix[0].reshape(1, -1),
        "win": w_in[0][:, :D_MAIN].astype(BF16),
        "wg": w_in[0][:, D_MAIN:].T,
        "wconv": w_conv[0],
        "bif": jnp.concatenate([b_i[0], b_f[0]]).reshape(1, 8),
        "gconv": g_conv[0].reshape(1, -1),
        "gmat": _group_matrix(),
        "gml": g_mlstm[0].reshape(1, -1),
        "wo": w_out[0].astype(BF16),
        "gffn": g_ffn[0].reshape(1, -1),
        "wpq": w_pq[0].T.astype(BF16),
        "keys": sub_keys[0].reshape(2 * PEER_HEADS, NKEYS, -1).astype(BF16),
        "u": u_tab[0].astype(BF16),
        "vt": v_tab[0].reshape(NEXP // EXPERT_ROWS, EXPERT_ROWS, D_MODEL).transpose(0, 2, 1).astype(BF16),
        "gfin": g_final.reshape(1, -1),
    }

    mod = _mod_rows(jnp.concatenate([c_prompt, c_sample], axis=0), w_mod[0], b_mod[0])
    mod_p = mod[:nb].reshape(nb, 1, 6 * D_MODEL)
    mod_s = jnp.broadcast_to(mod[nb:].reshape(ns, 1, 6 * D_MODEL), (ns, sseq, 6 * D_MODEL))
    mod_s = mod_s.reshape(ts, 6 * D_MODEL)
    mod_s_tiles = mod_s.reshape(ts // POST_ROWS, POST_ROWS, 6 * D_MODEL)

    xp = x_prompt.reshape(nb * seq, D_MODEL)
    ycn, qkv, og, gates, tail = _premix_prompt(xp, mod_p, w, nb, seq)
    hmn, c_p, n_p, m_p = _mlstm_prompt(qkv, og, gates, w["gml"], nb, seq)
    x1p, h2t_p = _postmix(xp, ycn, hmn, mod_p, w["wo"], w["gffn"], seq)
    conv_p = tail[:, 6:8][None]
    c_p = c_p[None]
    n_p = n_p[:, :, 0][None]
    m_p = m_p[:, :, 0, 0][None]

    xs = x_sample.reshape(ts, D_MODEL)
    st = state_conv[0]
    sm1 = jnp.pad(st[:, 1:2], ((0, 0), (0, sseq - 1), (0, 0))).reshape(ts, D_CONV)
    sm2 = jnp.pad(st, ((0, 0), (0, sseq - 2), (0, 0))).reshape(ts, D_CONV)
    ycn, qkv, og, gates, u_s = _premix_sample(xs, mod_s, sm1, sm2, w, sseq)

    def heads(a):
        return a.astype(F32).reshape(ns, sseq, HEADS, DH).transpose(0, 2, 1, 3).reshape(ns * HEADS, sseq, DH)

    q, k, v = (heads(qkv[:, i * D_MLSTM:(i + 1) * D_MLSTM]) for i in range(3))
    gate_t = gates.reshape(ns, sseq, 2, HEADS).transpose(2, 0, 3, 1).reshape(2, ns * HEADS, 1, sseq)
    gml_g = jnp.tile(g_mlstm[0].reshape(HEADS, 1, DH), (ns, 1, 1))
    hm, c_s, n_s, m_s = _mlstm_sample(
        q, k, v, heads(og), gate_t[0], gate_t[1], gml_g,
        state_mlstm_C[0].reshape(ns * HEADS, DH, DH), state_mlstm_n[0].reshape(ns * HEADS, 1, DH),
        state_mlstm_m[0].reshape(ns * HEADS, 1, 1), sseq)
    hmn = hm.reshape(ns, HEADS, sseq, DH).transpose(0, 2, 1, 3).reshape(ts, D_MLSTM).astype(BF16)
    x1s, h2t_s = _postmix(xs, ycn, hmn, mod_s_tiles, w["wo"], w["gffn"], POST_ROWS)

    y_p, p_s = _peer(h2t_p, h2t_s, x1p, mod_p, w["wpq"], w["keys"], w["u"], w["vt"], w["gfin"], seq)
    y_prompt = y_p.reshape(nb, seq, D_MODEL)
    y_sample = _final(x1s, p_s, mod_s_tiles, w["gfin"], POST_ROWS).reshape(ns, sseq, D_MODEL)
    conv_s = u_s.reshape(ns, sseq, D_CONV)[:, sseq - 2:][None]
    c_s = c_s.reshape(ns, HEADS, DH, DH)[None]
    n_s = n_s.reshape(ns, HEADS, DH)[None]
    m_s = m_s.reshape(ns, HEADS)[None]

    return (y_prompt, y_sample, conv_p, c_p, n_p, m_p, conv_s, c_s, n_s, m_s)
```

```python
import functools

import jax
import jax.numpy as jnp
import numpy as np
from jax import lax
from jax.experimental import pallas as pl
from jax.experimental.pallas import tpu as pltpu

F32 = jnp.float32
BF16 = jnp.bfloat16
NEG_INF = float("-inf")
EPS = 1e-6

D_MODEL = 1024
D_CONV = 512
CONV_GROUPS = 8
D_MLSTM = 512
HEADS = 4
DH = 128
D_MAIN = 3 * D_CONV + 4 * D_MLSTM
PEER_HEADS = 8
NKEYS = 128
TOPK = 16
NEXP = NKEYS * NKEYS

VMEM_LIMIT = 56 * 1024 * 1024
MOD_SH1, MOD_SC1, MOD_GT1, MOD_SH2, MOD_SC2, MOD_GT2 = range(6)

MOD_ROWS = 128
PREMIX_ROWS = 256
CHUNK = 256
MLSTM_SEQS = 1
POST_ROWS = 512
EXPERT_COLS = 512
EXPERT_ROWS = 1024
SAMPLE_GROUP = 32


def _dot(a, b):
    return jnp.dot(a, b, preferred_element_type=F32)


def _dot_nt(a, b):
    return lax.dot_general(a, b, (((1,), (1,)), ((), ())), preferred_element_type=F32)


def _dot_tn(a, b):
    return lax.dot_general(a, b, (((0,), (0,)), ((), ())), preferred_element_type=F32)


def _dot_f32(a, b):
    return jnp.dot(a, b, precision=lax.Precision.HIGHEST, preferred_element_type=F32)


def _params(*sem):
    return pltpu.CompilerParams(dimension_semantics=sem, vmem_limit_bytes=VMEM_LIMIT)


def _rms(x):
    return x * lax.rsqrt(jnp.mean(x * x, axis=-1, keepdims=True) + EPS)


def _mod_kernel(c_ref, w_ref, b_ref, o_ref):
    @pl.when(pl.program_id(0) == 0)
    def _():
        o_ref[...] = jnp.broadcast_to(b_ref[...], o_ref.shape)

    c = c_ref[...]
    a = (c * jax.nn.sigmoid(c)).astype(BF16)
    o_ref[...] += _dot(a, w_ref[...].astype(BF16))


def _mod_rows(c, w_mod, b_mod):
    n, width = c.shape[0], w_mod.shape[1]
    tk = MOD_ROWS
    return pl.pallas_call(
        _mod_kernel,
        grid=(D_MODEL // tk,),
        in_specs=[pl.BlockSpec((n, tk), lambda k: (0, k)),
                  pl.BlockSpec((tk, width), lambda k: (k, 0)),
                  _const_spec((1, width))],
        out_specs=_const_spec((n, width)),
        out_shape=jax.ShapeDtypeStruct((n, width), F32),
        compiler_params=_params("arbitrary"),
        name="mod_rows",
    )(c, w_mod, b_mod.reshape(1, -1))


def _log_sigmoid(x):
    return jnp.minimum(x, 0.0) - jnp.log1p(jnp.exp(-jnp.abs(x)))


def _premix_core(x, sh, sc, gmix, win_ref, wg_ref, bif):
    h = _rms(x) * gmix * (1.0 + sc) + sh
    z = _dot(h.astype(BF16), win_ref[...])
    lane = lax.broadcasted_iota(jnp.int32, (x.shape[0], 2 * HEADS), 1)
    pre = bif
    for j in range(2 * HEADS):
        col = jnp.sum(h * wg_ref[j:j + 1, :], axis=1, keepdims=True)
        pre = pre + jnp.where(lane == j, col, 0.0)
    gates = jnp.where(lane < HEADS, pre, _log_sigmoid(pre))
    return z, gates


def _conv_tail(z, um1, um2, u, wconv, gconv, gmat_ref, ycn_ref, qkv_ref, og_ref):
    bg = z[:, D_CONV:2 * D_CONV]
    conv = wconv[0:1] * um2 + wconv[1:2] * um1 + wconv[2:3] * u
    yc = bg * conv
    ysq = yc * yc
    hi = ysq.astype(BF16)
    lo = (ysq - hi.astype(F32)).astype(BF16)
    gsum = _dot(hi, gmat_ref[...]) + _dot(lo, gmat_ref[...])
    ycn_ref[...] = (yc * lax.rsqrt(gsum * (CONV_GROUPS / D_CONV) + EPS) * gconv).astype(BF16)
    o0 = 3 * D_CONV
    qkv_ref[:, 0:D_MLSTM] = z[:, o0:o0 + D_MLSTM].astype(BF16)
    qkv_ref[:, D_MLSTM:2 * D_MLSTM] = (z[:, o0 + D_MLSTM:o0 + 2 * D_MLSTM] * (DH ** -0.5)).astype(BF16)
    qkv_ref[:, 2 * D_MLSTM:3 * D_MLSTM] = z[:, o0 + 2 * D_MLSTM:o0 + 3 * D_MLSTM].astype(BF16)
    og_ref[...] = jax.nn.sigmoid(z[:, o0 + 3 * D_MLSTM:o0 + 4 * D_MLSTM]).astype(BF16)


def _premix_prompt_kernel(x_ref, sh_ref, sc_ref, gmix_ref, win_ref, wg_ref, wconv_ref, bif_ref, gconv_ref,
                          gmat_ref, ycn_ref, qkv_ref, og_ref, gate_ref, tail_ref, carry_ref):
    rows = x_ref.shape[0]

    @pl.when(pl.program_id(1) == 0)
    def _():
        carry_ref[...] = jnp.zeros_like(carry_ref)

    z, gates = _premix_core(x_ref[...], sh_ref[0], sc_ref[0], gmix_ref[...], win_ref, wg_ref, bif_ref[...])
    gate_ref[...] = gates
    u = z[:, 2 * D_CONV:3 * D_CONV] * z[:, 0:D_CONV]
    prev = carry_ref[...]
    p0, p1 = prev[6:7], prev[7:8]
    ri = lax.broadcasted_iota(jnp.int32, (rows, 1), 0)
    um1 = jnp.where(ri == 0, p1, pltpu.roll(u, 1, 0))
    um2 = jnp.where(ri == 0, p0, jnp.where(ri == 1, p1, pltpu.roll(u, 2, 0)))
    carry_ref[...] = u[rows - 8:rows]
    tail_ref[0] = u[rows - 8:rows]
    _conv_tail(z, um1, um2, u, wconv_ref[...], gconv_ref[...], gmat_ref, ycn_ref, qkv_ref, og_ref)


def _premix_sample_kernel(x_ref, sh_ref, sc_ref, gmix_ref, win_ref, wg_ref, wconv_ref, bif_ref, gconv_ref,
                          gmat_ref, sm1_ref, sm2_ref, ycn_ref, qkv_ref, og_ref, gate_ref, u_ref, *, seq):
    rows = x_ref.shape[0]
    z, gates = _premix_core(x_ref[...], sh_ref[0], sc_ref[0], gmix_ref[...], win_ref, wg_ref, bif_ref[...])
    gate_ref[...] = gates
    u = z[:, 2 * D_CONV:3 * D_CONV] * z[:, 0:D_CONV]
    u_ref[...] = u
    tmod = lax.broadcasted_iota(jnp.int32, (rows, 1), 0) % seq
    um1 = jnp.where(tmod == 0, sm1_ref[...], pltpu.roll(u, 1, 0))
    um2 = jnp.where(tmod < 2, sm2_ref[...], pltpu.roll(u, 2, 0))
    _conv_tail(z, um1, um2, u, wconv_ref[...], gconv_ref[...], gmat_ref, ycn_ref, qkv_ref, og_ref)


def _const_spec(shape):
    nd = len(shape)
    return pl.BlockSpec(shape, lambda *_: (0,) * nd)


def _premix_weights(w):
    return [w["gmix"], w["win"], w["wg"], w["wconv"], w["bif"], w["gconv"], w["gmat"]]


def _premix_weight_specs():
    return [_const_spec((1, D_MODEL)), _const_spec((D_MODEL, D_MAIN)), _const_spec((8, D_MODEL)),
            _const_spec((3, D_CONV)), _const_spec((1, 8)), _const_spec((1, D_CONV)),
            _const_spec((D_CONV, D_CONV))]


def _premix_prompt(x, mod, w, nb, seq):
    t = x.shape[0]
    rows = PREMIX_ROWS
    nl = seq // rows
    tok = lambda b, l: (b * nl + l, 0)
    per_seq = lambda col: pl.BlockSpec((1, 1, D_MODEL), lambda b, l: (b, 0, col))
    return pl.pallas_call(
        _premix_prompt_kernel,
        grid=(nb, nl),
        in_specs=[pl.BlockSpec((rows, D_MODEL), tok), per_seq(MOD_SH1), per_seq(MOD_SC1)] + _premix_weight_specs(),
        out_specs=[pl.BlockSpec((rows, D_CONV), tok), pl.BlockSpec((rows, 3 * D_MLSTM), tok),
                   pl.BlockSpec((rows, D_MLSTM), tok), pl.BlockSpec((rows, 8), tok),
                   pl.BlockSpec((1, 8, D_CONV), lambda b, l: (b, 0, 0))],
        out_shape=[jax.ShapeDtypeStruct((t, D_CONV), BF16), jax.ShapeDtypeStruct((t, 3 * D_MLSTM), BF16),
                   jax.ShapeDtypeStruct((t, D_MLSTM), BF16), jax.ShapeDtypeStruct((t, 8), F32),
                   jax.ShapeDtypeStruct((nb, 8, D_CONV), F32)],
        scratch_shapes=[pltpu.VMEM((8, D_CONV), F32)],
        compiler_params=_params("arbitrary", "arbitrary"),
        name="premix_prompt",
    )(x, mod, mod, *_premix_weights(w))


def _premix_sample(x, mod, sm1, sm2, w, seq):
    t = x.shape[0]
    full = lambda n: _const_spec((t, n))
    return pl.pallas_call(
        functools.partial(_premix_sample_kernel, seq=seq),
        grid=(1,),
        in_specs=[full(D_MODEL), pl.BlockSpec((1, t, D_MODEL), lambda i: (0, 0, MOD_SH1)),
                  pl.BlockSpec((1, t, D_MODEL), lambda i: (0, 0, MOD_SC1))] + _premix_weight_specs()
        + [full(D_CONV), full(D_CONV)],
        out_specs=[full(D_CONV), full(3 * D_MLSTM), full(D_MLSTM), full(8), full(D_CONV)],
        out_shape=[jax.ShapeDtypeStruct((t, D_CONV), BF16), jax.ShapeDtypeStruct((t, 3 * D_MLSTM), BF16),
                   jax.ShapeDtypeStruct((t, D_MLSTM), BF16), jax.ShapeDtypeStruct((t, 8), F32),
                   jax.ShapeDtypeStruct((t, D_CONV), F32)],
        compiler_params=_params("arbitrary"),
        name="premix_sample",
    )(x, mod, mod, *_premix_weights(w), sm1, sm2)


def _mlstm_chunk_kernel(qkv_ref, og_ref, gate_ref, gml_ref, h_ref, c_ref, n_ref, m_ref, c_scr, n_scr, m_scr):
    nseq, lc = qkv_ref.shape[0], qkv_ref.shape[1]

    @pl.when(pl.program_id(1) == 0)
    def _():
        c_scr[...] = jnp.zeros_like(c_scr)
        n_scr[...] = jnp.zeros_like(n_scr)
        m_scr[...] = jnp.zeros_like(m_scr)

    eye = (lax.broadcasted_iota(jnp.int32, (8, 8), 0) == lax.broadcasted_iota(jnp.int32, (8, 8), 1)).astype(F32)
    r = lax.broadcasted_iota(jnp.int32, (lc, lc), 0)
    c = lax.broadcasted_iota(jnp.int32, (lc, lc), 1)
    causal = c <= r
    tril = causal.astype(F32)

    for bb in range(nseq):
        gates = gate_ref[bb]
        grow = lax.dot_general(eye, gates, (((1,), (1,)), ((), ())), precision=lax.Precision.HIGHEST,
                               preferred_element_type=F32)
        bcol = _dot_f32(tril, gates)
        brow = lax.dot_general(grow, tril, (((1,), (1,)), ((), ())), precision=lax.Precision.HIGHEST,
                               preferred_element_type=F32)
        for h in range(HEADS):
            st = bb * HEADS + h
            q = qkv_ref[bb, :, h * DH:(h + 1) * DH]
            k = qkv_ref[bb, :, D_MLSTM + h * DH:D_MLSTM + (h + 1) * DH]
            v = qkv_ref[bb, :, 2 * D_MLSTM + h * DH:2 * D_MLSTM + (h + 1) * DH]
            li_c = gates[:, h:h + 1]
            li_r = grow[h:h + 1, :]
            b_c = bcol[:, HEADS + h:HEADS + h + 1]
            b_r = brow[HEADS + h:HEADS + h + 1, :]
            m_old = m_scr[st][0:1, 0:1]
            c_old = c_scr[st]
            n_old = n_scr[st][0:1, :]

            a = b_c + m_old
            dmat = jnp.where(causal, b_c - b_r + li_r, NEG_INF)
            mt = jnp.maximum(a, jnp.max(dmat, axis=1, keepdims=True))
            s = _dot_nt(q, k) * jnp.exp(dmat - mt)
            w_inter = jnp.exp(a - mt)
            num = w_inter * _dot(q, c_old.astype(BF16)) + _dot(s.astype(BF16), v)
            den = (w_inter * jnp.sum(q.astype(F32) * n_old, axis=1, keepdims=True)
                   + jnp.sum(s, axis=1, keepdims=True))
            hh = num / jnp.maximum(jnp.abs(den), jnp.exp(-mt))
            hh = og_ref[bb, :, h * DH:(h + 1) * DH].astype(F32) * hh
            h_ref[bb, :, h * DH:(h + 1) * DH] = (_rms(hh) * gml_ref[:, h * DH:(h + 1) * DH]).astype(BF16)

            bl = b_c[lc - 1:lc, :]
            gl = bl - b_c + li_c
            m_new = jnp.maximum(bl + m_old, jnp.max(gl, axis=0, keepdims=True))
            w_old = jnp.exp(bl + m_old - m_new)
            kw = k.astype(F32) * jnp.exp(gl - m_new)
            c_new = w_old * c_old + _dot_tn(kw.astype(BF16), v)
            n_new = w_old * n_old + jnp.sum(kw, axis=0, keepdims=True)
            c_scr[st] = c_new
            n_scr[st] = jnp.broadcast_to(n_new, (8, DH))
            m_scr[st] = jnp.broadcast_to(m_new, (8, DH))
            c_ref[bb, h] = c_new
            n_ref[bb, h] = jnp.broadcast_to(n_new, (8, DH))
            m_ref[bb, h] = jnp.broadcast_to(m_new, (8, DH))


def _mlstm_prompt(qkv, og, gates, gml, nb, seq):
    nc = seq // CHUNK
    ns = MLSTM_SEQS
    tok = lambda b, c: (b, c, 0)
    state = lambda b, c: (b, 0, 0, 0)
    per_seq = lambda a: a.reshape(nb, seq, a.shape[-1])
    hmn, c_p, n_p, m_p = pl.pallas_call(
        _mlstm_chunk_kernel,
        grid=(nb // ns, nc),
        in_specs=[pl.BlockSpec((ns, CHUNK, 3 * D_MLSTM), tok), pl.BlockSpec((ns, CHUNK, D_MLSTM), tok),
                  pl.BlockSpec((ns, CHUNK, 8), tok), _const_spec((1, D_MLSTM))],
        out_specs=[pl.BlockSpec((ns, CHUNK, D_MLSTM), tok), pl.BlockSpec((ns, HEADS, DH, DH), state),
                   pl.BlockSpec((ns, HEADS, 8, DH), state), pl.BlockSpec((ns, HEADS, 8, DH), state)],
        out_shape=[jax.ShapeDtypeStruct((nb, seq, D_MLSTM), BF16), jax.ShapeDtypeStruct((nb, HEADS, DH, DH), F32),
                   jax.ShapeDtypeStruct((nb, HEADS, 8, DH), F32), jax.ShapeDtypeStruct((nb, HEADS, 8, DH), F32)],
        scratch_shapes=[pltpu.VMEM((ns * HEADS, DH, DH), F32), pltpu.VMEM((ns * HEADS, 8, DH), F32),
                        pltpu.VMEM((ns * HEADS, 8, DH), F32)],
        compiler_params=_params("arbitrary", "arbitrary"),
        name="mlstm_prompt",
    )(per_seq(qkv), per_seq(og), per_seq(gates), gml)
    return hmn.reshape(nb * seq, D_MLSTM), c_p, n_p, m_p


def _mlstm_step_kernel(q_ref, k_ref, v_ref, og_ref, li_ref, lf_ref, gml_ref, c0_ref, n0_ref,
                       m0_ref, h_ref, c_ref, n_ref, m_ref, *, seq):
    qt_all = jnp.swapaxes(q_ref[...], 1, 2)
    kt_all = jnp.swapaxes(k_ref[...], 1, 2)
    c = c0_ref[...]
    n = n0_ref[...]
    m = m0_ref[...]
    for t in range(seq):
        lit = li_ref[:, :, t:t + 1]
        lft = lf_ref[:, :, t:t + 1]
        m_new = jnp.maximum(lft + m, lit)
        fw = jnp.exp(lft + m - m_new)
        iw = jnp.exp(lit - m_new)
        c = fw * c + (iw * kt_all[:, :, t:t + 1]) * v_ref[:, t:t + 1, :]
        n = fw * n + iw * k_ref[:, t:t + 1, :]
        num = jnp.sum(qt_all[:, :, t:t + 1] * c, axis=1, keepdims=True)
        den = jnp.sum(q_ref[:, t:t + 1, :] * n, axis=2, keepdims=True)
        hh = og_ref[:, t:t + 1, :] * (num / jnp.maximum(jnp.abs(den), jnp.exp(-m_new)))
        h_ref[:, t:t + 1, :] = _rms(hh) * gml_ref[...]
        m = m_new
    c_ref[...] = c
    n_ref[...] = n
    m_ref[...] = m


def _mlstm_sample(q, k, v, og, li, lf, gml, c0, n0, m0, seq):
    ng = q.shape[0]
    g = SAMPLE_GROUP
    blk = lambda *shape: pl.BlockSpec((g,) + shape, lambda i: (i,) + (0,) * len(shape))
    return pl.pallas_call(
        functools.partial(_mlstm_step_kernel, seq=seq),
        grid=(ng // g,),
        in_specs=[blk(seq, DH), blk(seq, DH), blk(seq, DH), blk(seq, DH),
                  blk(1, seq), blk(1, seq), blk(1, DH), blk(DH, DH), blk(1, DH), blk(1, 1)],
        out_specs=[blk(seq, DH), blk(DH, DH), blk(1, DH), blk(1, 1)],
        out_shape=[jax.ShapeDtypeStruct((ng, seq, DH), F32), jax.ShapeDtypeStruct((ng, DH, DH), F32),
                   jax.ShapeDtypeStruct((ng, 1, DH), F32), jax.ShapeDtypeStruct((ng, 1, 1), F32)],
        compiler_params=_params("arbitrary"),
        name="mlstm_sample",
    )(q, k, v, og, li, lf, gml, c0, n0, m0)


def _postmix_kernel(x_ref, ycn_ref, hmn_ref, gt_ref, sc_ref, sh_ref, wo_ref, gffn_ref, x1_ref, h2t_ref):
    y = _dot(ycn_ref[...], wo_ref[0:D_CONV, :]) + _dot(hmn_ref[...], wo_ref[D_CONV:D_CONV + D_MLSTM, :])
    x1 = x_ref[...] + gt_ref[0] * y
    x1_ref[...] = x1
    h2 = _rms(x1) * gffn_ref[...] * (1.0 + sc_ref[0]) + sh_ref[0]
    h2t_ref[...] = h2.T.astype(BF16)


def _postmix(x, ycn, hmn, mod, wo, gffn, rows_per_mod):
    t = x.shape[0]
    rows = POST_ROWS
    r = mod.shape[1]
    per = rows_per_mod // rows
    tok = lambda i: (i, 0)
    part = lambda col: pl.BlockSpec((1, r, D_MODEL), lambda i: (i // per, 0, col))
    return pl.pallas_call(
        _postmix_kernel,
        grid=(t // rows,),
        in_specs=[pl.BlockSpec((rows, D_MODEL), tok), pl.BlockSpec((rows, D_CONV), tok),
                  pl.BlockSpec((rows, D_MLSTM), tok), part(MOD_GT1), part(MOD_SC2), part(MOD_SH2),
                  _const_spec((D_MODEL, D_MODEL)), _const_spec((1, D_MODEL))],
        out_specs=[pl.BlockSpec((rows, D_MODEL), tok), pl.BlockSpec((D_MODEL, rows), lambda i: (0, i))],
        out_shape=[jax.ShapeDtypeStruct((t, D_MODEL), F32), jax.ShapeDtypeStruct((D_MODEL, t), BF16)],
        compiler_params=_params("arbitrary"),
        name="postmix",
    )(x, ycn, hmn, mod, mod, mod, wo, gffn)


def _odd_even_merge_sort_pairs(n):
    pairs = []

    def merge(lo, m, r):
        step = r * 2
        if step < m:
            merge(lo, m, step)
            merge(lo + r, m, step)
            pairs.extend((i, i + r) for i in range(lo + r, lo + m - r, step))
        else:
            pairs.append((lo, lo + r))

    def sort(lo, m):
        if m > 1:
            sort(lo, m // 2)
            sort(lo + m // 2, m // 2)
            merge(lo, m, 1)

    sort(0, n)
    return pairs


_SORT16 = _odd_even_merge_sort_pairs(TOPK)
_BITONIC16 = [(i, i + d) for d in (8, 4, 2, 1) for i in range(TOPK) if not i & d]
SUBLANES = 8


def _exchange(x, pairs):
    for i, j in pairs:
        if x[j] is None:
            continue
        if x[i] is None:
            x[i], x[j] = x[j], None
        else:
            x[i], x[j] = jnp.maximum(x[i], x[j]), jnp.minimum(x[i], x[j])


def _top16(blocks):
    x = list(blocks) + [None] * (TOPK - len(blocks))
    _exchange(x, _SORT16)
    for d in (4, 2, 1):
        p = [None if b is None else pltpu.roll(b, d, 0) for b in x]
        merged = []
        for r in range(TOPK):
            a, b = x[r], p[TOPK - 1 - r]
            merged.append(b if a is None else a if b is None else jnp.maximum(a, b))
        x = merged
        _exchange(x, _BITONIC16)
    return x


def _row_total(x):
    for d in (4, 2, 1):
        x = x + pltpu.roll(x, d, 0)
    return x


def _row_min(x):
    for d in (4, 2, 1):
        x = jnp.minimum(x, pltpu.roll(x, d, 0))
    return x


def _route_head(s1, s2):
    cols = s1.shape[1]
    nb = NKEYS // SUBLANES
    b1 = [s1[SUBLANES * r:SUBLANES * (r + 1)] for r in range(nb)]
    b2 = [s2[SUBLANES * r:SUBLANES * (r + 1)] for r in range(nb)]
    t1 = _top16(b1)
    t2 = _top16(b2)
    sub = lax.broadcasted_iota(jnp.int32, (SUBLANES, cols), 0)
    t2_lo, t2_hi, t1_hi = t2[0], t2[8], t1[8]
    for b in range(1, SUBLANES):
        t2_lo = jnp.where(sub == b, t2[b], t2_lo)
        t2_hi = jnp.where(sub == b, t2[8 + b], t2_hi)
        t1_hi = jnp.where(sub == b, t1[8 + b], t1_hi)
    stair = [t1[0] + t2_lo, t1[0] + t2_hi]
    stair += [jnp.where(sub < TOPK // (a + 1), t1[a] + t2_lo, NEG_INF) for a in range(1, 8)]
    stair += [t1_hi + t2[0]]
    tau = _top16(stair)[TOPK - 1]
    top = t1[0] + t2[0]
    z = sum(jnp.where(cand >= tau, jnp.exp(cand - top), 0.0) for cand in stair)
    inv_z = 1.0 / _row_total(z)
    e2t_lo, e2t_hi = jnp.exp(t2_lo - t2[0]), jnp.exp(t2_hi - t2[0])
    inf = float("inf")
    thr = [_row_min(jnp.minimum(jnp.where(stair[0] >= tau, e2t_lo, inf), jnp.where(stair[1] >= tau, e2t_hi, inf)))]
    thr += [_row_min(jnp.where(cand >= tau, e2t_lo, inf)) for cand in stair[2:9]]
    e2t_top = jnp.exp(t2[0] - t2[0])
    thr += [jnp.where(t1[a] + t2[0] >= tau, e2t_top, inf) for a in range(8, TOPK)]
    thr1, c1, e2 = [], [], []
    for r in range(nb):
        c = jnp.full((SUBLANES, cols), inf, F32)
        for a in reversed(range(TOPK)):
            c = jnp.where(b1[r] == t1[a], thr[a], c)
        thr1.append(c)
        c1.append(jnp.exp(b1[r] - t1[0]) * (0.5 * inv_z))
        e2.append(jnp.exp(b2[r] - t2[0]))
    return thr1, c1, e2


def _gelu_tanh_x2(a):
    c = 0.7978845608028654
    t = jnp.tanh(a * (c + (c * 0.044715) * (a * a)))
    return a + a * t


PACK = 16
LANES = 128


PEER_STEPS = NEXP // (2 * EXPERT_ROWS)
GATE_BLOCKS = 8
VALUE_MATMUL_AFTER = (1, 4)


def _peer_kernel(h2tp_ref, h2ts_ref, wpq_ref, keys_ref, u_ref, vta_ref, vtb_ref, x1_ref, gt_ref, gfin_ref,
                 y_ref, ps_ref, thr1_ref, c1_ref, e2_ref, acc_ref, ga_ref, gb_ref, ht_ref, *, prompt_tiles):
    i = pl.program_id(0)
    s = pl.program_id(1)
    cols = ht_ref.shape[1]

    @pl.when(jnp.logical_and(s == 0, i < prompt_tiles))
    def _():
        ht_ref[...] = h2tp_ref[...]

    @pl.when(jnp.logical_and(s == 0, i >= prompt_tiles))
    def _():
        ht_ref[...] = h2ts_ref[...]

    ht = ht_ref[...]

    @pl.when(s == 0)
    def _():
        acc_ref[...] = jnp.zeros_like(acc_ref)
        gb_ref[...] = jnp.zeros_like(gb_ref)

        def head(h, carry):
            row0 = pl.multiple_of(h * 2 * NKEYS, 2 * NKEYS)
            qt = _dot(wpq_ref[pl.ds(row0, 2 * NKEYS), :], ht).astype(BF16)
            s1 = _dot(keys_ref[h], qt[0:NKEYS])
            s2 = _dot(keys_ref[PEER_HEADS + h], qt[NKEYS:2 * NKEYS])
            thr1, c1, e2 = _route_head(s1, s2)
            for r in range(NKEYS // SUBLANES):
                for cc in range(cols // LANES):
                    lanes = slice(cc * LANES, (cc + 1) * LANES)
                    thr1_ref[r, cc, h] = thr1[r][:, lanes]
                    c1_ref[r, cc, h] = c1[r][:, lanes]
                    e2_ref[h, cc, SUBLANES * r:SUBLANES * (r + 1), :] = e2[r][:, lanes]
            return carry

        lax.fori_loop(0, PEER_HEADS, head, 0)

    n_i1 = EXPERT_ROWS // NKEYS

    def build(tile, half, g_ref, vt_ref, g_done_ref):
        for j in range(n_i1):
            u0 = half * EXPERT_ROWS + j * NKEYS
            a = _dot(u_ref[u0:u0 + NKEYS, :], ht)
            if j in VALUE_MATMUL_AFTER:
                i = VALUE_MATMUL_AFTER.index(j)
                n = D_MODEL // len(VALUE_MATMUL_AFTER)
                acc_ref[i * n:(i + 1) * n, :] += _dot(vt_ref[0, i * n:(i + 1) * n, :], g_done_ref[...])
            for cc in range(cols // LANES):
                lanes = slice(cc * LANES, (cc + 1) * LANES)
                for kh in range(0, NKEYS // SUBLANES, GATE_BLOCKS):
                    w = [None] * GATE_BLOCKS
                    for h in range(PEER_HEADS):
                        thr = thr1_ref[tile, cc, h, j:j + 1, :]
                        c1 = c1_ref[tile, cc, h, j:j + 1, :]
                        for k in range(GATE_BLOCKS):
                            e2 = e2_ref[h, cc, (kh + k) * SUBLANES:(kh + k + 1) * SUBLANES, :]
                            term = jnp.where(e2 >= thr, e2, 0.0) * c1
                            w[k] = term if h == 0 else w[k] + term
                    for k in range(0, GATE_BLOCKS, 2):
                        r0 = (kh + k) * SUBLANES
                        gate = jnp.concatenate(w[k:k + 2], axis=0)
                        g_ref[j * NKEYS + r0:j * NKEYS + r0 + PACK, lanes] = (
                            gate * _gelu_tanh_x2(a[r0:r0 + PACK, lanes])).astype(BF16)

    @pl.when(s < PEER_STEPS)
    def _():
        build(2 * s, 0, ga_ref, vta_ref, gb_ref)
        build(2 * s + 1, 1, gb_ref, vtb_ref, ga_ref)

    @pl.when(s == PEER_STEPS)
    def _():
        acc_ref[...] += _dot(vta_ref[0], gb_ref[...])

    @pl.when(jnp.logical_and(s == PEER_STEPS, i < prompt_tiles))
    def _():
        y_ref[...] = _rms(x1_ref[...] + gt_ref[0] * acc_ref[...].T) * gfin_ref[...]

    @pl.when(jnp.logical_and(s == PEER_STEPS, i >= prompt_tiles))
    def _():
        ps_ref[...] = acc_ref[...].T


def _peer(h2t_p, h2t_s, x1_p, mod_p, wpq, keys, u, vt, gfin, seq):
    tp = h2t_p.shape[1]
    cols = EXPERT_COLS
    assert h2t_s.shape[1] == cols and tp % cols == 0 and seq % cols == 0
    n_p = tp // cols
    last = NEXP // EXPERT_ROWS - 1
    p_tile = lambda i: jnp.minimum(i, n_p - 1)
    return pl.pallas_call(
        functools.partial(_peer_kernel, prompt_tiles=n_p),
        grid=(n_p + 1, PEER_STEPS + 1),
        in_specs=[pl.BlockSpec((D_MODEL, cols), lambda i, s: (0, p_tile(i))),
                  _const_spec((D_MODEL, cols)),
                  _const_spec((PEER_HEADS * 2 * NKEYS, D_MODEL)),
                  _const_spec((2 * PEER_HEADS, NKEYS, NKEYS)),
                  pl.BlockSpec((2 * EXPERT_ROWS, D_MODEL), lambda i, s: (jnp.minimum(s, PEER_STEPS - 1), 0)),
                  pl.BlockSpec((1, D_MODEL, EXPERT_ROWS), lambda i, s: (jnp.maximum(2 * s - 1, 0), 0, 0)),
                  pl.BlockSpec((1, D_MODEL, EXPERT_ROWS), lambda i, s: (jnp.minimum(2 * s, last), 0, 0)),
                  pl.BlockSpec((cols, D_MODEL), lambda i, s: (p_tile(i), 0)),
                  pl.BlockSpec((1, 1, D_MODEL), lambda i, s: (p_tile(i) // (seq // cols), 0, MOD_GT2)),
                  _const_spec((1, D_MODEL))],
        out_specs=[pl.BlockSpec((cols, D_MODEL), lambda i, s: (p_tile(i), 0)), _const_spec((cols, D_MODEL))],
        out_shape=[jax.ShapeDtypeStruct((tp, D_MODEL), F32), jax.ShapeDtypeStruct((cols, D_MODEL), F32)],
        scratch_shapes=[pltpu.VMEM((NKEYS // SUBLANES, cols // LANES, PEER_HEADS, SUBLANES, LANES), F32),
                        pltpu.VMEM((NKEYS // SUBLANES, cols // LANES, PEER_HEADS, SUBLANES, LANES), F32),
                        pltpu.VMEM((PEER_HEADS, cols // LANES, NKEYS, LANES), F32),
                        pltpu.VMEM((D_MODEL, cols), F32),
                        pltpu.VMEM((EXPERT_ROWS, cols), BF16), pltpu.VMEM((EXPERT_ROWS, cols), BF16),
                        pltpu.VMEM((D_MODEL, cols), BF16)],
        compiler_params=_params("arbitrary", "arbitrary"),
        name="peer",
    )(h2t_p, h2t_s, wpq, keys, u, vt, vt, x1_p, mod_p, gfin)


def _final_kernel(x1_ref, p_ref, gt_ref, gfin_ref, y_ref):
    y_ref[...] = _rms(x1_ref[...] + gt_ref[0] * p_ref[...]) * gfin_ref[...]


def _final(x1, p, mod, gfin, rows_per_mod):
    t = x1.shape[0]
    rows = POST_ROWS
    r = mod.shape[1]
    per = rows_per_mod // rows
    tok = pl.BlockSpec((rows, D_MODEL), lambda i: (i, 0))
    return pl.pallas_call(
        _final_kernel,
        grid=(t // rows,),
        in_specs=[tok, tok, pl.BlockSpec((1, r, D_MODEL), lambda i: (i // per, 0, MOD_GT2)),
                  _const_spec((1, D_MODEL))],
        out_specs=tok,
        out_shape=jax.ShapeDtypeStruct((t, D_MODEL), F32),
        compiler_params=_params("arbitrary"),
        name="final_norm",
    )(x1, p, mod, gfin)


def _group_matrix():
    g = np.arange(D_CONV) // (D_CONV // CONV_GROUPS)
    return jnp.asarray(g[:, None] == g[None, :], dtype=BF16)


def kernel(x_prompt, x_sample, c_prompt, c_sample, state_conv, state_mlstm_C, state_mlstm_n, state_mlstm_m,
           w_mod, b_mod, g_mix, w_in, w_conv, b_i, b_f, g_conv, g_mlstm, w_out, g_ffn, w_pq, sub_keys,
           u_tab, v_tab, g_final):
    depth = w_mod.shape[0]
    assert depth == 1
    nb, seq, _ = x_prompt.shape
    ns, sseq, _ = x_sample.shape
    ts = ns * sseq

    w = {
        "gmix": g_mix[0].reshape(1, -1),
        "win": w_in[0][:, :D_MAIN].astype(BF16),
        "wg": w_in[0][:, D_MAIN:].T,
        "wconv": w_conv[0],
        "bif": jnp.concatenate([b_i[0], b_f[0]]).reshape(1, 8),
        "gconv": g_conv[0].reshape(1, -1),
        "gmat": _group_matrix(),
        "gml": g_mlstm[0].reshape(1, -1),
        "wo": w_out[0].astype(BF16),
        "gffn": g_ffn[0].reshape(1, -1),
        "wpq": w_pq[0].T.astype(BF16),
        "keys": sub_keys[0].reshape(2 * PEER_HEADS, NKEYS, -1).astype(BF16),
        "u": u_tab[0].astype(BF16),
        "vt": v_tab[0].reshape(NEXP // EXPERT_ROWS, EXPERT_ROWS, D_MODEL).transpose(0, 2, 1).astype(BF16),
        "gfin": g_final.reshape(1, -1),
    }

    mod = _mod_rows(jnp.concatenate([c_prompt, c_sample], axis=0), w_mod[0], b_mod[0])
    mod_p = mod[:nb].reshape(nb, 1, 6 * D_MODEL)
    assert ts == POST_ROWS
    mod_s = jnp.broadcast_to(mod[nb:].reshape(ns, 1, 6 * D_MODEL), (ns, sseq, 6 * D_MODEL))
    mod_s_tiles = mod_s.reshape(1, ts, 6 * D_MODEL)

    xp = x_prompt.reshape(nb * seq, D_MODEL)
    ycn, qkv, og, gates, tail = _premix_prompt(xp, mod_p, w, nb, seq)
    hmn, c_p, n_p, m_p = _mlstm_prompt(qkv, og, gates, w["gml"], nb, seq)
    x1p, h2t_p = _postmix(xp, ycn, hmn, mod_p, w["wo"], w["gffn"], seq)
    conv_p = tail[:, 6:8][None]
    c_p = c_p[None]
    n_p = n_p[:, :, 0][None]
    m_p = m_p[:, :, 0, 0][None]

    xs = x_sample.reshape(ts, D_MODEL)
    st = state_conv[0]
    sm1 = jnp.pad(st[:, 1:2], ((0, 0), (0, sseq - 1), (0, 0))).reshape(ts, D_CONV)
    sm2 = jnp.pad(st, ((0, 0), (0, sseq - 2), (0, 0))).reshape(ts, D_CONV)
    ycn, qkv, og, gates, u_s = _premix_sample(xs, mod_s_tiles, sm1, sm2, w, sseq)

    def heads(a):
        return a.astype(F32).reshape(ns, sseq, HEADS, DH).transpose(0, 2, 1, 3).reshape(ns * HEADS, sseq, DH)

    q, k, v = (heads(qkv[:, i * D_MLSTM:(i + 1) * D_MLSTM]) for i in range(3))
    gate_t = gates.reshape(ns, sseq, 2, HEADS).transpose(2, 0, 3, 1).reshape(2, ns * HEADS, 1, sseq)
    gml_g = jnp.tile(g_mlstm[0].reshape(HEADS, 1, DH), (ns, 1, 1))
    hm, c_s, n_s, m_s = _mlstm_sample(
        q, k, v, heads(og), gate_t[0], gate_t[1], gml_g,
        state_mlstm_C[0].reshape(ns * HEADS, DH, DH), state_mlstm_n[0].reshape(ns * HEADS, 1, DH),
        state_mlstm_m[0].reshape(ns * HEADS, 1, 1), sseq)
    hmn = hm.reshape(ns, HEADS, sseq, DH).transpose(0, 2, 1, 3).reshape(ts, D_MLSTM).astype(BF16)
    x1s, h2t_s = _postmix(xs, ycn, hmn, mod_s_tiles, w["wo"], w["gffn"], POST_ROWS)

    y_p, p_s = _peer(h2t_p, h2t_s, x1p, mod_p, w["wpq"], w["keys"], w["u"], w["vt"], w["gfin"], seq)
    y_prompt = y_p.reshape(nb, seq, D_MODEL)
    y_sample = _final(x1s, p_s, mod_s_tiles, w["gfin"], POST_ROWS).reshape(ns, sseq, D_MODEL)
    conv_s = u_s.reshape(ns, sseq, D_CONV)[:, sseq - 2:][None]
    c_s = c_s.reshape(ns, HEADS, DH, DH)[None]
    n_s = n_s.reshape(ns, HEADS, DH)[None]
    m_s = m_s.reshape(ns, HEADS)[None]

    return (y_prompt, y_sample, conv_p, c_p, n_p, m_p, conv_s, c_s, n_s, m_s)
```

```python
import functools

import jax
import jax.numpy as jnp
import numpy as np
from jax import lax
from jax.experimental import pallas as pl
from jax.experimental.pallas import tpu as pltpu

F32 = jnp.float32
BF16 = jnp.bfloat16
NEG_INF = float("-inf")
EPS = 1e-6

D_MODEL = 1024
D_CONV = 512
CONV_GROUPS = 8
D_MLSTM = 512
HEADS = 4
DH = 128
D_MAIN = 3 * D_CONV + 4 * D_MLSTM
PEER_HEADS = 8
NKEYS = 128
TOPK = 16
NEXP = NKEYS * NKEYS

VMEM_LIMIT = 56 * 1024 * 1024
MOD_SH1, MOD_SC1, MOD_GT1, MOD_SH2, MOD_SC2, MOD_GT2 = range(6)

MOD_ROWS = 128
PREMIX_ROWS = 256
CHUNK = 256
MLSTM_SEQS = 1
POST_ROWS = 512
EXPERT_COLS = 512
EXPERT_ROWS = 1024
SAMPLE_GROUP = 32


def _dot(a, b):
    return jnp.dot(a, b, preferred_element_type=F32)


def _dot_nt(a, b):
    return lax.dot_general(a, b, (((1,), (1,)), ((), ())), preferred_element_type=F32)


def _dot_tn(a, b):
    return lax.dot_general(a, b, (((0,), (0,)), ((), ())), preferred_element_type=F32)


def _dot_f32(a, b):
    return jnp.dot(a, b, precision=lax.Precision.HIGHEST, preferred_element_type=F32)


def _params(*sem):
    return pltpu.CompilerParams(dimension_semantics=sem, vmem_limit_bytes=VMEM_LIMIT)


def _rms(x):
    return x * lax.rsqrt(jnp.mean(x * x, axis=-1, keepdims=True) + EPS)


def _mod_kernel(c_ref, w_ref, b_ref, o_ref):
    @pl.when(pl.program_id(0) == 0)
    def _():
        o_ref[...] = jnp.broadcast_to(b_ref[...], o_ref.shape)

    c = c_ref[...]
    a = (c * jax.nn.sigmoid(c)).astype(BF16)
    o_ref[...] += _dot(a, w_ref[...].astype(BF16))


def _mod_rows(c, w_mod, b_mod):
    n, width = c.shape[0], w_mod.shape[1]
    tk = MOD_ROWS
    return pl.pallas_call(
        _mod_kernel,
        grid=(D_MODEL // tk,),
        in_specs=[pl.BlockSpec((n, tk), lambda k: (0, k)),
                  pl.BlockSpec((tk, width), lambda k: (k, 0)),
                  _const_spec((1, width))],
        out_specs=_const_spec((n, width)),
        out_shape=jax.ShapeDtypeStruct((n, width), F32),
        compiler_params=_params("arbitrary"),
        name="mod_rows",
    )(c, w_mod, b_mod.reshape(1, -1))


def _log_sigmoid(x):
    return jnp.minimum(x, 0.0) - jnp.log1p(jnp.exp(-jnp.abs(x)))


def _premix_core(x, sh, sc, gmix, win_ref, wg_ref, bif):
    h = _rms(x) * gmix * (1.0 + sc) + sh
    z = _dot(h.astype(BF16), win_ref[...])
    lane = lax.broadcasted_iota(jnp.int32, (x.shape[0], 2 * HEADS), 1)
    pre = bif
    for j in range(2 * HEADS):
        col = jnp.sum(h * wg_ref[j:j + 1, :], axis=1, keepdims=True)
        pre = pre + jnp.where(lane == j, col, 0.0)
    gates = jnp.where(lane < HEADS, pre, _log_sigmoid(pre))
    return z, gates


def _conv_tail(z, um1, um2, u, wconv, gconv, gmat_ref, ycn_ref, qkv_ref, og_ref):
    bg = z[:, D_CONV:2 * D_CONV]
    conv = wconv[0:1] * um2 + wconv[1:2] * um1 + wconv[2:3] * u
    yc = bg * conv
    ysq = yc * yc
    hi = ysq.astype(BF16)
    lo = (ysq - hi.astype(F32)).astype(BF16)
    gsum = _dot(hi, gmat_ref[...]) + _dot(lo, gmat_ref[...])
    ycn_ref[...] = (yc * lax.rsqrt(gsum * (CONV_GROUPS / D_CONV) + EPS) * gconv).astype(BF16)
    o0 = 3 * D_CONV
    qkv_ref[:, 0:D_MLSTM] = z[:, o0:o0 + D_MLSTM].astype(BF16)
    qkv_ref[:, D_MLSTM:2 * D_MLSTM] = (z[:, o0 + D_MLSTM:o0 + 2 * D_MLSTM] * (DH ** -0.5)).astype(BF16)
    qkv_ref[:, 2 * D_MLSTM:3 * D_MLSTM] = z[:, o0 + 2 * D_MLSTM:o0 + 3 * D_MLSTM].astype(BF16)
    og_ref[...] = jax.nn.sigmoid(z[:, o0 + 3 * D_MLSTM:o0 + 4 * D_MLSTM]).astype(BF16)


def _premix_prompt_kernel(x_ref, sh_ref, sc_ref, gmix_ref, win_ref, wg_ref, wconv_ref, bif_ref, gconv_ref,
                          gmat_ref, ycn_ref, qkv_ref, og_ref, gate_ref, tail_ref, carry_ref):
    rows = x_ref.shape[0]

    @pl.when(pl.program_id(1) == 0)
    def _():
        carry_ref[...] = jnp.zeros_like(carry_ref)

    z, gates = _premix_core(x_ref[...], sh_ref[0], sc_ref[0], gmix_ref[...], win_ref, wg_ref, bif_ref[...])
    gate_ref[...] = gates
    u = z[:, 2 * D_CONV:3 * D_CONV] * z[:, 0:D_CONV]
    prev = carry_ref[...]
    p0, p1 = prev[6:7], prev[7:8]
    ri = lax.broadcasted_iota(jnp.int32, (rows, 1), 0)
    um1 = jnp.where(ri == 0, p1, pltpu.roll(u, 1, 0))
    um2 = jnp.where(ri == 0, p0, jnp.where(ri == 1, p1, pltpu.roll(u, 2, 0)))
    carry_ref[...] = u[rows - 8:rows]
    tail_ref[0] = u[rows - 8:rows]
    _conv_tail(z, um1, um2, u, wconv_ref[...], gconv_ref[...], gmat_ref, ycn_ref, qkv_ref, og_ref)


def _premix_sample_kernel(x_ref, sh_ref, sc_ref, gmix_ref, win_ref, wg_ref, wconv_ref, bif_ref, gconv_ref,
                          gmat_ref, sm1_ref, sm2_ref, ycn_ref, qkv_ref, og_ref, gate_ref, u_ref, *, seq):
    rows = x_ref.shape[0]
    z, gates = _premix_core(x_ref[...], sh_ref[0], sc_ref[0], gmix_ref[...], win_ref, wg_ref, bif_ref[...])
    gate_ref[...] = gates
    u = z[:, 2 * D_CONV:3 * D_CONV] * z[:, 0:D_CONV]
    u_ref[...] = u
    tmod = lax.broadcasted_iota(jnp.int32, (rows, 1), 0) % seq
    um1 = jnp.where(tmod == 0, sm1_ref[...], pltpu.roll(u, 1, 0))
    um2 = jnp.where(tmod < 2, sm2_ref[...], pltpu.roll(u, 2, 0))
    _conv_tail(z, um1, um2, u, wconv_ref[...], gconv_ref[...], gmat_ref, ycn_ref, qkv_ref, og_ref)


def _const_spec(shape):
    nd = len(shape)
    return pl.BlockSpec(shape, lambda *_: (0,) * nd)


def _premix_weights(w):
    return [w["gmix"], w["win"], w["wg"], w["wconv"], w["bif"], w["gconv"], w["gmat"]]


def _premix_weight_specs():
    return [_const_spec((1, D_MODEL)), _const_spec((D_MODEL, D_MAIN)), _const_spec((8, D_MODEL)),
            _const_spec((3, D_CONV)), _const_spec((1, 8)), _const_spec((1, D_CONV)),
            _const_spec((D_CONV, D_CONV))]


def _premix_prompt(x, mod, w, nb, seq):
    t = x.shape[0]
    rows = PREMIX_ROWS
    nl = seq // rows
    tok = lambda b, l: (b * nl + l, 0)
    per_seq = lambda col: pl.BlockSpec((1, 1, D_MODEL), lambda b, l: (b, 0, col))
    return pl.pallas_call(
        _premix_prompt_kernel,
        grid=(nb, nl),
        in_specs=[pl.BlockSpec((rows, D_MODEL), tok), per_seq(MOD_SH1), per_seq(MOD_SC1)] + _premix_weight_specs(),
        out_specs=[pl.BlockSpec((rows, D_CONV), tok), pl.BlockSpec((rows, 3 * D_MLSTM), tok),
                   pl.BlockSpec((rows, D_MLSTM), tok), pl.BlockSpec((rows, 8), tok),
                   pl.BlockSpec((1, 8, D_CONV), lambda b, l: (b, 0, 0))],
        out_shape=[jax.ShapeDtypeStruct((t, D_CONV), BF16), jax.ShapeDtypeStruct((t, 3 * D_MLSTM), BF16),
                   jax.ShapeDtypeStruct((t, D_MLSTM), BF16), jax.ShapeDtypeStruct((t, 8), F32),
                   jax.ShapeDtypeStruct((nb, 8, D_CONV), F32)],
        scratch_shapes=[pltpu.VMEM((8, D_CONV), F32)],
        compiler_params=_params("arbitrary", "arbitrary"),
        name="premix_prompt",
    )(x, mod, mod, *_premix_weights(w))


def _premix_sample(x, mod, sm1, sm2, w, seq):
    t = x.shape[0]
    full = lambda n: _const_spec((t, n))
    return pl.pallas_call(
        functools.partial(_premix_sample_kernel, seq=seq),
        grid=(1,),
        in_specs=[full(D_MODEL), pl.BlockSpec((1, t, D_MODEL), lambda i: (0, 0, MOD_SH1)),
                  pl.BlockSpec((1, t, D_MODEL), lambda i: (0, 0, MOD_SC1))] + _premix_weight_specs()
        + [full(D_CONV), full(D_CONV)],
        out_specs=[full(D_CONV), full(3 * D_MLSTM), full(D_MLSTM), full(8), full(D_CONV)],
        out_shape=[jax.ShapeDtypeStruct((t, D_CONV), BF16), jax.ShapeDtypeStruct((t, 3 * D_MLSTM), BF16),
                   jax.ShapeDtypeStruct((t, D_MLSTM), BF16), jax.ShapeDtypeStruct((t, 8), F32),
                   jax.ShapeDtypeStruct((t, D_CONV), F32)],
        compiler_params=_params("arbitrary"),
        name="premix_sample",
    )(x, mod, mod, *_premix_weights(w), sm1, sm2)


def _mlstm_chunk_kernel(qkv_ref, og_ref, gate_ref, gml_ref, h_ref, c_ref, n_ref, m_ref, c_scr, n_scr, m_scr):
    nseq, lc = qkv_ref.shape[0], qkv_ref.shape[1]

    @pl.when(pl.program_id(1) == 0)
    def _():
        c_scr[...] = jnp.zeros_like(c_scr)
        n_scr[...] = jnp.zeros_like(n_scr)
        m_scr[...] = jnp.zeros_like(m_scr)

    eye = (lax.broadcasted_iota(jnp.int32, (8, 8), 0) == lax.broadcasted_iota(jnp.int32, (8, 8), 1)).astype(F32)
    r = lax.broadcasted_iota(jnp.int32, (lc, lc), 0)
    c = lax.broadcasted_iota(jnp.int32, (lc, lc), 1)
    causal = c <= r
    tril = causal.astype(F32)

    for bb in range(nseq):
        gates = gate_ref[bb]
        grow = lax.dot_general(eye, gates, (((1,), (1,)), ((), ())), precision=lax.Precision.HIGHEST,
                               preferred_element_type=F32)
        bcol = _dot_f32(tril, gates)
        brow = lax.dot_general(grow, tril, (((1,), (1,)), ((), ())), precision=lax.Precision.HIGHEST,
                               preferred_element_type=F32)
        for h in range(HEADS):
            st = bb * HEADS + h
            q = qkv_ref[bb, :, h * DH:(h + 1) * DH]
            k = qkv_ref[bb, :, D_MLSTM + h * DH:D_MLSTM + (h + 1) * DH]
            v = qkv_ref[bb, :, 2 * D_MLSTM + h * DH:2 * D_MLSTM + (h + 1) * DH]
            li_c = gates[:, h:h + 1]
            li_r = grow[h:h + 1, :]
            b_c = bcol[:, HEADS + h:HEADS + h + 1]
            b_r = brow[HEADS + h:HEADS + h + 1, :]
            m_old = m_scr[st][0:1, 0:1]
            c_old = c_scr[st]
            n_old = n_scr[st][0:1, :]

            a = b_c + m_old
            dmat = jnp.where(causal, b_c - b_r + li_r, NEG_INF)
            mt = jnp.maximum(a, jnp.max(dmat, axis=1, keepdims=True))
            s = _dot_nt(q, k) * jnp.exp(dmat - mt)
            w_inter = jnp.exp(a - mt)
            num = w_inter * _dot(q, c_old.astype(BF16)) + _dot(s.astype(BF16), v)
            den = (w_inter * jnp.sum(q.astype(F32) * n_old, axis=1, keepdims=True)
                   + jnp.sum(s, axis=1, keepdims=True))
            hh = num / jnp.maximum(jnp.abs(den), jnp.exp(-mt))
            hh = og_ref[bb, :, h * DH:(h + 1) * DH].astype(F32) * hh
            h_ref[bb, :, h * DH:(h + 1) * DH] = (_rms(hh) * gml_ref[:, h * DH:(h + 1) * DH]).astype(BF16)

            bl = b_c[lc - 1:lc, :]
            gl = bl - b_c + li_c
            m_new = jnp.maximum(bl + m_old, jnp.max(gl, axis=0, keepdims=True))
            w_old = jnp.exp(bl + m_old - m_new)
            kw = k.astype(F32) * jnp.exp(gl - m_new)
            c_new = w_old * c_old + _dot_tn(kw.astype(BF16), v)
            n_new = w_old * n_old + jnp.sum(kw, axis=0, keepdims=True)
            c_scr[st] = c_new
            n_scr[st] = jnp.broadcast_to(n_new, (8, DH))
            m_scr[st] = jnp.broadcast_to(m_new, (8, DH))
            c_ref[bb, h] = c_new
            n_ref[bb, h] = jnp.broadcast_to(n_new, (8, DH))
            m_ref[bb, h] = jnp.broadcast_to(m_new, (8, DH))


def _mlstm_prompt(qkv, og, gates, gml, nb, seq):
    nc = seq // CHUNK
    ns = MLSTM_SEQS
    tok = lambda b, c: (b, c, 0)
    state = lambda b, c: (b, 0, 0, 0)
    per_seq = lambda a: a.reshape(nb, seq, a.shape[-1])
    hmn, c_p, n_p, m_p = pl.pallas_call(
        _mlstm_chunk_kernel,
        grid=(nb // ns, nc),
        in_specs=[pl.BlockSpec((ns, CHUNK, 3 * D_MLSTM), tok), pl.BlockSpec((ns, CHUNK, D_MLSTM), tok),
                  pl.BlockSpec((ns, CHUNK, 8), tok), _const_spec((1, D_MLSTM))],
        out_specs=[pl.BlockSpec((ns, CHUNK, D_MLSTM), tok), pl.BlockSpec((ns, HEADS, DH, DH), state),
                   pl.BlockSpec((ns, HEADS, 8, DH), state), pl.BlockSpec((ns, HEADS, 8, DH), state)],
        out_shape=[jax.ShapeDtypeStruct((nb, seq, D_MLSTM), BF16), jax.ShapeDtypeStruct((nb, HEADS, DH, DH), F32),
                   jax.ShapeDtypeStruct((nb, HEADS, 8, DH), F32), jax.ShapeDtypeStruct((nb, HEADS, 8, DH), F32)],
        scratch_shapes=[pltpu.VMEM((ns * HEADS, DH, DH), F32), pltpu.VMEM((ns * HEADS, 8, DH), F32),
                        pltpu.VMEM((ns * HEADS, 8, DH), F32)],
        compiler_params=_params("arbitrary", "arbitrary"),
        name="mlstm_prompt",
    )(per_seq(qkv), per_seq(og), per_seq(gates), gml)
    return hmn.reshape(nb * seq, D_MLSTM), c_p, n_p, m_p


def _mlstm_step_kernel(q_ref, k_ref, v_ref, og_ref, li_ref, lf_ref, gml_ref, c0_ref, n0_ref,
                       m0_ref, h_ref, c_ref, n_ref, m_ref, *, seq):
    kt_all = jnp.swapaxes(k_ref[...], 1, 2)
    c = c0_ref[...]
    n = n0_ref[...]
    m = m0_ref[...]
    for t in range(seq):
        lit = li_ref[:, :, t:t + 1]
        lft = lf_ref[:, :, t:t + 1]
        m_new = jnp.maximum(lft + m, lit)
        fw = jnp.exp(lft + m - m_new)
        iw = jnp.exp(lit - m_new)
        c = fw * c + (iw * kt_all[:, :, t:t + 1]) * v_ref[:, t:t + 1, :]
        n = fw * n + iw * k_ref[:, t:t + 1, :]
        num = lax.dot_general(q_ref[:, t:t + 1, :], c, (((2,), (1,)), ((0,), (0,))),
                              preferred_element_type=F32)
        den = jnp.sum(q_ref[:, t:t + 1, :] * n, axis=2, keepdims=True)
        hh = og_ref[:, t:t + 1, :] * (num / jnp.maximum(jnp.abs(den), jnp.exp(-m_new)))
        h_ref[:, t:t + 1, :] = _rms(hh) * gml_ref[...]
        m = m_new
    c_ref[...] = c
    n_ref[...] = n
    m_ref[...] = m


def _mlstm_sample(q, k, v, og, li, lf, gml, c0, n0, m0, seq):
    ng = q.shape[0]
    g = SAMPLE_GROUP
    blk = lambda *shape: pl.BlockSpec((g,) + shape, lambda i: (i,) + (0,) * len(shape))
    return pl.pallas_call(
        functools.partial(_mlstm_step_kernel, seq=seq),
        grid=(ng // g,),
        in_specs=[blk(seq, DH), blk(seq, DH), blk(seq, DH), blk(seq, DH),
                  blk(1, seq), blk(1, seq), blk(1, DH), blk(DH, DH), blk(1, DH), blk(1, 1)],
        out_specs=[blk(seq, DH), blk(DH, DH), blk(1, DH), blk(1, 1)],
        out_shape=[jax.ShapeDtypeStruct((ng, seq, DH), F32), jax.ShapeDtypeStruct((ng, DH, DH), F32),
                   jax.ShapeDtypeStruct((ng, 1, DH), F32), jax.ShapeDtypeStruct((ng, 1, 1), F32)],
        compiler_params=_params("arbitrary"),
        name="mlstm_sample",
    )(q, k, v, og, li, lf, gml, c0, n0, m0)


def _postmix_kernel(x_ref, ycn_ref, hmn_ref, gt_ref, sc_ref, sh_ref, wo_ref, gffn_ref, x1_ref, h2t_ref):
    y = _dot(ycn_ref[...], wo_ref[0:D_CONV, :]) + _dot(hmn_ref[...], wo_ref[D_CONV:D_CONV + D_MLSTM, :])
    x1 = x_ref[...] + gt_ref[0] * y
    x1_ref[...] = x1
    h2 = _rms(x1) * gffn_ref[...] * (1.0 + sc_ref[0]) + sh_ref[0]
    h2t_ref[...] = h2.T.astype(BF16)


def _postmix(x, ycn, hmn, mod, wo, gffn, rows_per_mod):
    t = x.shape[0]
    rows = POST_ROWS
    r = mod.shape[1]
    per = rows_per_mod // rows
    tok = lambda i: (i, 0)
    part = lambda col: pl.BlockSpec((1, r, D_MODEL), lambda i: (i // per, 0, col))
    return pl.pallas_call(
        _postmix_kernel,
        grid=(t // rows,),
        in_specs=[pl.BlockSpec((rows, D_MODEL), tok), pl.BlockSpec((rows, D_CONV), tok),
                  pl.BlockSpec((rows, D_MLSTM), tok), part(MOD_GT1), part(MOD_SC2), part(MOD_SH2),
                  _const_spec((D_MODEL, D_MODEL)), _const_spec((1, D_MODEL))],
        out_specs=[pl.BlockSpec((rows, D_MODEL), tok), pl.BlockSpec((D_MODEL, rows), lambda i: (0, i))],
        out_shape=[jax.ShapeDtypeStruct((t, D_MODEL), F32), jax.ShapeDtypeStruct((D_MODEL, t), BF16)],
        compiler_params=_params("arbitrary"),
        name="postmix",
    )(x, ycn, hmn, mod, mod, mod, wo, gffn)


def _odd_even_merge_sort_pairs(n):
    pairs = []

    def merge(lo, m, r):
        step = r * 2
        if step < m:
            merge(lo, m, step)
            merge(lo + r, m, step)
            pairs.extend((i, i + r) for i in range(lo + r, lo + m - r, step))
        else:
            pairs.append((lo, lo + r))

    def sort(lo, m):
        if m > 1:
            sort(lo, m // 2)
            sort(lo + m // 2, m // 2)
            merge(lo, m, 1)

    sort(0, n)
    return pairs


_SORT16 = _odd_even_merge_sort_pairs(TOPK)
_BITONIC16 = [(i, i + d) for d in (8, 4, 2, 1) for i in range(TOPK) if not i & d]
SUBLANES = 8


def _exchange(x, pairs):
    for i, j in pairs:
        if x[j] is None:
            continue
        if x[i] is None:
            x[i], x[j] = x[j], None
        else:
            x[i], x[j] = jnp.maximum(x[i], x[j]), jnp.minimum(x[i], x[j])


def _top16(blocks):
    x = list(blocks) + [None] * (TOPK - len(blocks))
    _exchange(x, _SORT16)
    for d in (4, 2, 1):
        p = [None if b is None else pltpu.roll(b, d, 0) for b in x]
        merged = []
        for r in range(TOPK):
            a, b = x[r], p[TOPK - 1 - r]
            merged.append(b if a is None else a if b is None else jnp.maximum(a, b))
        x = merged
        _exchange(x, _BITONIC16)
    return x


def _row_total(x):
    for d in (4, 2, 1):
        x = x + pltpu.roll(x, d, 0)
    return x


def _row_min(x):
    for d in (4, 2, 1):
        x = jnp.minimum(x, pltpu.roll(x, d, 0))
    return x


def _route_head(s1, s2):
    cols = s1.shape[1]
    nb = NKEYS // SUBLANES
    b1 = [s1[SUBLANES * r:SUBLANES * (r + 1)] for r in range(nb)]
    b2 = [s2[SUBLANES * r:SUBLANES * (r + 1)] for r in range(nb)]
    t1 = _top16(b1)
    t2 = _top16(b2)
    sub = lax.broadcasted_iota(jnp.int32, (SUBLANES, cols), 0)
    t2_lo, t2_hi, t1_hi = t2[0], t2[8], t1[8]
    for b in range(1, SUBLANES):
        t2_lo = jnp.where(sub == b, t2[b], t2_lo)
        t2_hi = jnp.where(sub == b, t2[8 + b], t2_hi)
        t1_hi = jnp.where(sub == b, t1[8 + b], t1_hi)
    stair = [t1[0] + t2_lo, t1[0] + t2_hi]
    stair += [jnp.where(sub < TOPK // (a + 1), t1[a] + t2_lo, NEG_INF) for a in range(1, 8)]
    stair += [t1_hi + t2[0]]
    tau = _top16(stair)[TOPK - 1]
    top = t1[0] + t2[0]
    z = sum(jnp.where(cand >= tau, jnp.exp(cand - top), 0.0) for cand in stair)
    inv_z = 1.0 / _row_total(z)
    e2t_lo, e2t_hi = jnp.exp(t2_lo - t2[0]), jnp.exp(t2_hi - t2[0])
    inf = float("inf")
    thr = [_row_min(jnp.minimum(jnp.where(stair[0] >= tau, e2t_lo, inf), jnp.where(stair[1] >= tau, e2t_hi, inf)))]
    thr += [_row_min(jnp.where(cand >= tau, e2t_lo, inf)) for cand in stair[2:9]]
    e2t_top = jnp.exp(t2[0] - t2[0])
    thr += [jnp.where(t1[a] + t2[0] >= tau, e2t_top, inf) for a in range(8, TOPK)]
    thr1, c1, e2 = [], [], []
    for r in range(nb):
        c = jnp.full((SUBLANES, cols), inf, F32)
        for a in reversed(range(TOPK)):
            c = jnp.where(b1[r] == t1[a], thr[a], c)
        thr1.append(c)
        c1.append(jnp.exp(b1[r] - t1[0]) * (0.5 * inv_z))
        e2.append(jnp.exp(b2[r] - t2[0]))
    return thr1, c1, e2


def _gelu_tanh_x2(a):
    c = 0.7978845608028654
    t = jnp.tanh(a * (c + (c * 0.044715) * (a * a)))
    return a + a * t


PACK = 16
LANES = 128


PEER_STEPS = NEXP // (2 * EXPERT_ROWS)
GATE_BLOCKS = 8
VALUE_MATMUL_AFTER = (1, 4)


def _peer_kernel(h2tp_ref, h2ts_ref, wpq_ref, keys_ref, u_ref, vta_ref, vtb_ref, x1_ref, gt_ref, gfin_ref,
                 y_ref, ps_ref, thr1_ref, c1_ref, e2_ref, acc_ref, ga_ref, gb_ref, ht_ref, *, prompt_tiles):
    i = pl.program_id(0)
    s = pl.program_id(1)
    cols = ht_ref.shape[1]

    @pl.when(jnp.logical_and(s == 0, i < prompt_tiles))
    def _():
        ht_ref[...] = h2tp_ref[...]

    @pl.when(jnp.logical_and(s == 0, i >= prompt_tiles))
    def _():
        ht_ref[...] = h2ts_ref[...]

    ht = ht_ref[...]

    @pl.when(s == 0)
    def _():
        acc_ref[...] = jnp.zeros_like(acc_ref)
        gb_ref[...] = jnp.zeros_like(gb_ref)

        def head(h, carry):
            row0 = pl.multiple_of(h * 2 * NKEYS, 2 * NKEYS)
            qt = _dot(wpq_ref[pl.ds(row0, 2 * NKEYS), :], ht).astype(BF16)
            s1 = _dot(keys_ref[h], qt[0:NKEYS])
            s2 = _dot(keys_ref[PEER_HEADS + h], qt[NKEYS:2 * NKEYS])
            thr1, c1, e2 = _route_head(s1, s2)
            for r in range(NKEYS // SUBLANES):
                for cc in range(cols // LANES):
                    lanes = slice(cc * LANES, (cc + 1) * LANES)
                    thr1_ref[r, cc, h] = thr1[r][:, lanes]
                    c1_ref[r, cc, h] = c1[r][:, lanes]
                    e2_ref[h, cc, SUBLANES * r:SUBLANES * (r + 1), :] = e2[r][:, lanes]
            return carry

        lax.fori_loop(0, PEER_HEADS, head, 0)

    n_i1 = EXPERT_ROWS // NKEYS

    def build(tile, half, g_ref, vt_ref, g_done_ref):
        for j in range(n_i1):
            u0 = half * EXPERT_ROWS + j * NKEYS
            a = _dot(u_ref[u0:u0 + NKEYS, :], ht)
            if j in VALUE_MATMUL_AFTER:
                i = VALUE_MATMUL_AFTER.index(j)
                n = D_MODEL // len(VALUE_MATMUL_AFTER)
                acc_ref[i * n:(i + 1) * n, :] += _dot(vt_ref[0, i * n:(i + 1) * n, :], g_done_ref[...])
            for cc in range(cols // LANES):
                lanes = slice(cc * LANES, (cc + 1) * LANES)
                for kh in range(0, NKEYS // SUBLANES, GATE_BLOCKS):
                    w = [None] * GATE_BLOCKS
                    for h in range(PEER_HEADS):
                        thr = thr1_ref[tile, cc, h, j:j + 1, :]
                        c1 = c1_ref[tile, cc, h, j:j + 1, :]
                        for k in range(GATE_BLOCKS):
                            e2 = e2_ref[h, cc, (kh + k) * SUBLANES:(kh + k + 1) * SUBLANES, :]
                            term = jnp.where(e2 >= thr, e2, 0.0) * c1
                            w[k] = term if h == 0 else w[k] + term
                    for k in range(0, GATE_BLOCKS, 2):
                        r0 = (kh + k) * SUBLANES
                        gate = jnp.concatenate(w[k:k + 2], axis=0)
                        g_ref[j * NKEYS + r0:j * NKEYS + r0 + PACK, lanes] = (
                            gate * _gelu_tanh_x2(a[r0:r0 + PACK, lanes])).astype(BF16)

    @pl.when(s < PEER_STEPS)
    def _():
        build(2 * s, 0, ga_ref, vta_ref, gb_ref)
        build(2 * s + 1, 1, gb_ref, vtb_ref, ga_ref)

    @pl.when(s == PEER_STEPS)
    def _():
        acc_ref[...] += _dot(vta_ref[0], gb_ref[...])

    @pl.when(jnp.logical_and(s == PEER_STEPS, i < prompt_tiles))
    def _():
        y_ref[...] = _rms(x1_ref[...] + gt_ref[0] * acc_ref[...].T) * gfin_ref[...]

    @pl.when(jnp.logical_and(s == PEER_STEPS, i >= prompt_tiles))
    def _():
        ps_ref[...] = acc_ref[...].T


def _peer(h2t_p, h2t_s, x1_p, mod_p, wpq, keys, u, vt, gfin, seq):
    tp = h2t_p.shape[1]
    cols = EXPERT_COLS
    assert h2t_s.shape[1] == cols and tp % cols == 0 and seq % cols == 0
    n_p = tp // cols
    last = NEXP // EXPERT_ROWS - 1
    p_tile = lambda i: jnp.minimum(i, n_p - 1)
    return pl.pallas_call(
        functools.partial(_peer_kernel, prompt_tiles=n_p),
        grid=(n_p + 1, PEER_STEPS + 1),
        in_specs=[pl.BlockSpec((D_MODEL, cols), lambda i, s: (0, p_tile(i))),
                  _const_spec((D_MODEL, cols)),
                  _const_spec((PEER_HEADS * 2 * NKEYS, D_MODEL)),
                  _const_spec((2 * PEER_HEADS, NKEYS, NKEYS)),
                  pl.BlockSpec((2 * EXPERT_ROWS, D_MODEL), lambda i, s: (jnp.minimum(s, PEER_STEPS - 1), 0)),
                  pl.BlockSpec((1, D_MODEL, EXPERT_ROWS), lambda i, s: (jnp.maximum(2 * s - 1, 0), 0, 0)),
                  pl.BlockSpec((1, D_MODEL, EXPERT_ROWS), lambda i, s: (jnp.minimum(2 * s, last), 0, 0)),
                  pl.BlockSpec((cols, D_MODEL), lambda i, s: (p_tile(i), 0)),
                  pl.BlockSpec((1, 1, D_MODEL), lambda i, s: (p_tile(i) // (seq // cols), 0, MOD_GT2)),
                  _const_spec((1, D_MODEL))],
        out_specs=[pl.BlockSpec((cols, D_MODEL), lambda i, s: (p_tile(i), 0)), _const_spec((cols, D_MODEL))],
        out_shape=[jax.ShapeDtypeStruct((tp, D_MODEL), F32), jax.ShapeDtypeStruct((cols, D_MODEL), F32)],
        scratch_shapes=[pltpu.VMEM((NKEYS // SUBLANES, cols // LANES, PEER_HEADS, SUBLANES, LANES), F32),
                        pltpu.VMEM((NKEYS // SUBLANES, cols // LANES, PEER_HEADS, SUBLANES, LANES), F32),
                        pltpu.VMEM((PEER_HEADS, cols // LANES, NKEYS, LANES), F32),
                        pltpu.VMEM((D_MODEL, cols), F32),
                        pltpu.VMEM((EXPERT_ROWS, cols), BF16), pltpu.VMEM((EXPERT_ROWS, cols), BF16),
                        pltpu.VMEM((D_MODEL, cols), BF16)],
        compiler_params=_params("arbitrary", "arbitrary"),
        name="peer",
    )(h2t_p, h2t_s, wpq, keys, u, vt, vt, x1_p, mod_p, gfin)


def _final_kernel(x1_ref, p_ref, gt_ref, gfin_ref, y_ref):
    y_ref[...] = _rms(x1_ref[...] + gt_ref[0] * p_ref[...]) * gfin_ref[...]


def _final(x1, p, mod, gfin, rows_per_mod):
    t = x1.shape[0]
    rows = POST_ROWS
    r = mod.shape[1]
    per = rows_per_mod // rows
    tok = pl.BlockSpec((rows, D_MODEL), lambda i: (i, 0))
    return pl.pallas_call(
        _final_kernel,
        grid=(t // rows,),
        in_specs=[tok, tok, pl.BlockSpec((1, r, D_MODEL), lambda i: (i // per, 0, MOD_GT2)),
                  _const_spec((1, D_MODEL))],
        out_specs=tok,
        out_shape=jax.ShapeDtypeStruct((t, D_MODEL), F32),
        compiler_params=_params("arbitrary"),
        name="final_norm",
    )(x1, p, mod, gfin)


def _group_matrix():
    g = np.arange(D_CONV) // (D_CONV // CONV_GROUPS)
    return jnp.asarray(g[:, None] == g[None, :], dtype=BF16)


def kernel(x_prompt, x_sample, c_prompt, c_sample, state_conv, state_mlstm_C, state_mlstm_n, state_mlstm_m,
           w_mod, b_mod, g_mix, w_in, w_conv, b_i, b_f, g_conv, g_mlstm, w_out, g_ffn, w_pq, sub_keys,
           u_tab, v_tab, g_final):
    depth = w_mod.shape[0]
    assert depth == 1
    nb, seq, _ = x_prompt.shape
    ns, sseq, _ = x_sample.shape
    ts = ns * sseq

    w = {
        "gmix": g_mix[0].reshape(1, -1),
        "win": w_in[0][:, :D_MAIN].astype(BF16),
        "wg": w_in[0][:, D_MAIN:].T,
        "wconv": w_conv[0],
        "bif": jnp.concatenate([b_i[0], b_f[0]]).reshape(1, 8),
        "gconv": g_conv[0].reshape(1, -1),
        "gmat": _group_matrix(),
        "gml": g_mlstm[0].reshape(1, -1),
        "wo": w_out[0].astype(BF16),
        "gffn": g_ffn[0].reshape(1, -1),
        "wpq": w_pq[0].T.astype(BF16),
        "keys": sub_keys[0].reshape(2 * PEER_HEADS, NKEYS, -1).astype(BF16),
        "u": u_tab[0].astype(BF16),
        "vt": v_tab[0].reshape(NEXP // EXPERT_ROWS, EXPERT_ROWS, D_MODEL).transpose(0, 2, 1).astype(BF16),
        "gfin": g_final.reshape(1, -1),
    }

    mod = _mod_rows(jnp.concatenate([c_prompt, c_sample], axis=0), w_mod[0], b_mod[0])
    mod_p = mod[:nb].reshape(nb, 1, 6 * D_MODEL)
    assert ts == POST_ROWS
    mod_s = jnp.broadcast_to(mod[nb:].reshape(ns, 1, 6 * D_MODEL), (ns, sseq, 6 * D_MODEL))
    mod_s_tiles = mod_s.reshape(1, ts, 6 * D_MODEL)

    xp = x_prompt.reshape(nb * seq, D_MODEL)
    ycn, qkv, og, gates, tail = _premix_prompt(xp, mod_p, w, nb, seq)
    hmn, c_p, n_p, m_p = _mlstm_prompt(qkv, og, gates, w["gml"], nb, seq)
    x1p, h2t_p = _postmix(xp, ycn, hmn, mod_p, w["wo"], w["gffn"], seq)
    conv_p = tail[:, 6:8][None]
    c_p = c_p[None]
    n_p = n_p[:, :, 0][None]
    m_p = m_p[:, :, 0, 0][None]

    xs = x_sample.reshape(ts, D_MODEL)
    st = state_conv[0]
    sm1 = jnp.pad(st[:, 1:2], ((0, 0), (0, sseq - 1), (0, 0))).reshape(ts, D_CONV)
    sm2 = jnp.pad(st, ((0, 0), (0, sseq - 2), (0, 0))).reshape(ts, D_CONV)
    ycn, qkv, og, gates, u_s = _premix_sample(xs, mod_s_tiles, sm1, sm2, w, sseq)

    def heads(a):
        return a.astype(F32).reshape(ns, sseq, HEADS, DH).transpose(0, 2, 1, 3).reshape(ns * HEADS, sseq, DH)

    q, k, v = (heads(qkv[:, i * D_MLSTM:(i + 1) * D_MLSTM]) for i in range(3))
    gate_t = gates.reshape(ns, sseq, 2, HEADS).transpose(2, 0, 3, 1).reshape(2, ns * HEADS, 1, sseq)
    gml_g = jnp.tile(g_mlstm[0].reshape(HEADS, 1, DH), (ns, 1, 1))
    hm, c_s, n_s, m_s = _mlstm_sample(
        q, k, v, heads(og), gate_t[0], gate_t[1], gml_g,
        state_mlstm_C[0].reshape(ns * HEADS, DH, DH), state_mlstm_n[0].reshape(ns * HEADS, 1, DH),
        state_mlstm_m[0].reshape(ns * HEADS, 1, 1), sseq)
    hmn = hm.reshape(ns, HEADS, sseq, DH).transpose(0, 2, 1, 3).reshape(ts, D_MLSTM).astype(BF16)
    x1s, h2t_s = _postmix(xs, ycn, hmn, mod_s_tiles, w["wo"], w["gffn"], POST_ROWS)

    y_p, p_s = _peer(h2t_p, h2t_s, x1p, mod_p, w["wpq"], w["keys"], w["u"], w["vt"], w["gfin"], seq)
    y_prompt = y_p.reshape(nb, seq, D_MODEL)
    y_sample = _final(x1s, p_s, mod_s_tiles, w["gfin"], POST_ROWS).reshape(ns, sseq, D_MODEL)
    conv_s = u_s.reshape(ns, sseq, D_CONV)[:, sseq - 2:][None]
    c_s = c_s.reshape(ns, HEADS, DH, DH)[None]
    n_s = n_s.reshape(ns, HEADS, DH)[None]
    m_s = m_s.reshape(ns, HEADS)[None]

    return (y_prompt, y_sample, conv_p, c_p, n_p, m_p, conv_s, c_s, n_s, m_s)
```
